```python
import math
import jax, jax.numpy as jnp
from jax import lax
import numpy as np

D_MODEL = 1024
BATCH = 8
SEQ = 2048
DEPTH = 2

N_BRANCH = 3
BRANCH_WIDTH = 512
ATTN_Q_HEADS = 8
ATTN_KV_HEADS = 2
ATTN_Q_PER_KV = ATTN_Q_HEADS // ATTN_KV_HEADS
ATTN_HEAD_DIM = 64
ATTN_WINDOW = 128
ATTN_BLOCK = 128
ROPE_DIM = ATTN_HEAD_DIM // 4
ROPE_THETA = 500000.0
DN_HEADS = 8
DN_KEY_DIM = 64
DN_VALUE_DIM = 64
DN_CONV = 4
DN_CHUNK = 64
DN_QKV_WIDTH = DN_HEADS * (2 * DN_KEY_DIM + DN_VALUE_DIM)
RET_HEADS = 4
RET_KEY_DIM = 64
RET_VALUE_DIM = 128
RET_CHUNK = 64
RET_THETA = 10000.0
D_FF = 2816
N_MOD = 9
EPS = 1e-6
NEG_INF = -1e30

IN_SIZES = (
    ATTN_Q_HEADS * ATTN_HEAD_DIM,
    ATTN_KV_HEADS * ATTN_HEAD_DIM,
    ATTN_KV_HEADS * ATTN_HEAD_DIM,
    DN_QKV_WIDTH,
    DN_HEADS,
    DN_HEADS,
    DN_HEADS * DN_VALUE_DIM,
    RET_HEADS * RET_KEY_DIM,
    RET_HEADS * RET_KEY_DIM,
    RET_HEADS * RET_VALUE_DIM,
    RET_HEADS * RET_VALUE_DIM,
    N_BRANCH * D_MODEL,
)
N_IN = 768 + 2064 + 1536 + 3 * D_MODEL

kernel_name = 'hybrid_gated_parallel_mixer'


def _split_columns(t, sizes):
    out, start = [], 0
    for n in sizes:
        out.append(t[..., start:start + n])
        start += n
    return out


def rms_norm(x):
    xf = x.astype(jnp.float32)
    y = xf * lax.rsqrt(jnp.mean(xf * xf, axis=-1, keepdims=True) + EPS)
    return y.astype(x.dtype)


def l2_norm(x):
    xf = x.astype(jnp.float32)
    return (xf * lax.rsqrt(jnp.sum(xf * xf, axis=-1, keepdims=True) + EPS)).astype(x.dtype)


def modulate(x, gain, shift, scale):
    return rms_norm(x) * gain * (1.0 + scale) + shift


def swiglu_ffn(u, w13, w2):
    gate, up = jnp.split(u @ w13, 2, axis=-1)
    return (jax.nn.silu(gate) * up) @ w2


def partial_rope(x, pos):
    half = ROPE_DIM // 2
    inv_freq = ROPE_THETA ** (-jnp.arange(0, ROPE_DIM, 2, dtype=jnp.float32) / ROPE_DIM)
    phase = pos[:, None] * inv_freq[None, :]
    cos = jnp.cos(phase)[None, :, None, :].astype(x.dtype)
    sin = jnp.sin(phase)[None, :, None, :].astype(x.dtype)
    x1, x2, rest = x[..., :half], x[..., half:ROPE_DIM], x[..., ROPE_DIM:]
    return jnp.concatenate([x1 * cos - x2 * sin, x2 * cos + x1 * sin, rest], axis=-1)


def retnet_rotate(x, pos):
    dk = x.shape[-1]
    angle = 1.0 / (RET_THETA ** jnp.linspace(0.0, 1.0, dk // 2, dtype=jnp.float32))
    angle = jnp.repeat(angle, 2)
    phase = pos[:, None] * angle[None, :]
    cos = jnp.cos(phase)[None, :, None, :].astype(x.dtype)
    sin = jnp.sin(phase)[None, :, None, :].astype(x.dtype)
    rot = jnp.stack([-x[..., 1::2], x[..., 0::2]], axis=-1).reshape(x.shape)
    return x * cos + rot * sin


def causal_depthwise_conv(x, w):
    k = w.shape[0]
    return lax.conv_general_dilated(
        x, w[:, None, :].astype(x.dtype), window_strides=(1,), padding=[(k - 1, 0)],
        dimension_numbers=('NWC', 'WIO', 'NWC'), feature_group_count=x.shape[-1])


def sliding_window_attention(q, k, v, sinks):
    b, s = q.shape[0], q.shape[1]
    nb = s // ATTN_BLOCK
    qb = q.reshape(b, nb, ATTN_BLOCK, ATTN_KV_HEADS, ATTN_Q_PER_KV, ATTN_HEAD_DIM)

    def band(t):
        tb = t.reshape(b, nb, ATTN_BLOCK, ATTN_KV_HEADS, ATTN_HEAD_DIM)
        prev = jnp.pad(tb, ((0, 0), (1, 0), (0, 0), (0, 0), (0, 0)))[:, :-1]
        return jnp.concatenate([prev, tb], axis=2)

    kb, vb = band(k), band(v)
    logits = jnp.einsum('bnqgrd,bnkgd->bngrqk', qb, kb).astype(jnp.float32) * (ATTN_HEAD_DIM ** -0.5)
    qi = jnp.arange(ATTN_BLOCK)[:, None] + ATTN_BLOCK
    kj = jnp.arange(2 * ATTN_BLOCK)[None, :]
    in_window = (kj <= qi) & (kj > qi - ATTN_WINDOW)
    key_abs = jnp.arange(nb)[:, None, None] * ATTN_BLOCK + kj[None] - ATTN_BLOCK
    valid = in_window[None] & (key_abs >= 0)
    logits = jnp.where(valid[None, :, None, None], logits, NEG_INF)
    sink = sinks.astype(jnp.float32).reshape(ATTN_KV_HEADS, ATTN_Q_PER_KV)[None, None, :, :, None, None]
    sink = jnp.broadcast_to(sink, logits.shape[:-1] + (1,))
    probs = jax.nn.softmax(jnp.concatenate([logits, sink], axis=-1), axis=-1)[..., :-1]
    out = jnp.einsum('bngrqk,bnkgd->bnqgrd', probs.astype(v.dtype), vb)
    return out.reshape(b, s, ATTN_Q_HEADS * ATTN_HEAD_DIM)


def gated_delta_rule_chunked(q, k, v, log_decay, beta):
    b, s, h, dk = q.shape
    dv = v.shape[-1]
    c = DN_CHUNK
    nc = s // c
    f32 = jnp.float32

    def to_chunks(t):
        return t.astype(f32).reshape(b, nc, c, h, -1).transpose(0, 3, 1, 2, 4)

    q = to_chunks(q) * (dk ** -0.5)
    k = to_chunks(k)
    v = to_chunks(v)
    g = jnp.cumsum(to_chunks(log_decay[..., None])[..., 0], axis=-1)
    beta = to_chunks(beta[..., None])
    causal = jnp.tril(jnp.ones((c, c), dtype=bool))
    strict = jnp.tril(jnp.ones((c, c), dtype=bool), k=-1)
    diff = g[..., :, None] - g[..., None, :]
    decay = jnp.where(causal, jnp.exp(jnp.where(causal, diff, 0.0)), 0.0)
    k_beta = k * beta
    v_beta = v * beta
    lower = jnp.where(strict, jnp.einsum('bhnid,bhnjd->bhnij', k_beta, k) * decay, 0.0)
    unit_lower = lower + jnp.eye(c, dtype=f32)
    rhs = jnp.concatenate([v_beta, k_beta * jnp.exp(g)[..., None]], axis=-1)
    sol = lax.linalg.triangular_solve(unit_lower, rhs, left_side=True, lower=True, unit_diagonal=True)
    u_c, w_c = sol[..., :dv], sol[..., dv:]
    intra = jnp.where(causal, jnp.einsum('bhnid,bhnjd->bhnij', q, k) * decay, 0.0)
    q_decayed = q * jnp.exp(g)[..., None]
    k_to_end = k * jnp.exp(g[..., -1:] - g)[..., None]
    chunk_decay = jnp.exp(g[..., -1])

    def step(state, xs):
        u_i, w_i, intra_i, qd_i, kt_i, dec_i = xs
        v_new = u_i - jnp.einsum('bhik,bhkv->bhiv', w_i, state)
        out = jnp.einsum('bhik,bhkv->bhiv', qd_i, state) + jnp.einsum('bhij,bhjv->bhiv', intra_i, v_new)
        state = state * dec_i[..., None, None] + jnp.einsum('bhik,bhiv->bhkv', kt_i, v_new)
        return state, out

    xs = tuple(jnp.moveaxis(t, 2, 0) for t in (u_c, w_c, intra, q_decayed, k_to_end, chunk_decay))
    _, out = lax.scan(step, jnp.zeros((b, h, dk, dv), f32), xs)
    return out.transpose(1, 0, 3, 2, 4).reshape(b, s, h, dv)


def retention_chunked(q, k, v):
    b, s, h, dk = q.shape
    dv = v.shape[-1]
    c = RET_CHUNK
    nc = s // c
    f32 = jnp.float32

    def to_chunks(t):
        return t.astype(f32).reshape(b, nc, c, h, -1).transpose(0, 3, 1, 2, 4)

    q, k, v = to_chunks(q), to_chunks(k), to_chunks(v)
    log_gamma = jnp.log1p(-jnp.exp2(-5.0 - jnp.arange(h, dtype=f32)))
    pos = jnp.arange(c, dtype=f32)
    causal = jnp.tril(jnp.ones((c, c), dtype=bool))
    diff = jnp.where(causal, pos[:, None] - pos[None, :], 0.0)
    decay = jnp.where(causal, jnp.exp(log_gamma[:, None, None] * diff), 0.0)
    scores = jnp.einsum('bhnid,bhnjd->bhnij', q, k) * decay[None, :, None]
    intra_out = jnp.einsum('bhnij,bhnjv->bhniv', scores, v)
    q_decayed = q * jnp.exp(log_gamma[:, None] * (pos + 1.0))[None, :, None, :, None]
    k_to_end = k * jnp.exp(log_gamma[:, None] * (c - 1.0 - pos))[None, :, None, :, None]
    chunk_kv = jnp.einsum('bhnjd,bhnjv->bhndv', k_to_end, v)
    chunk_decay = jnp.exp(log_gamma * c)[None, :, None, None]

    def step(state, xs):
        qd_i, kv_i = xs
        out = jnp.einsum('bhik,bhkv->bhiv', qd_i, state)
        return state * chunk_decay + kv_i, out

    _, inter = lax.scan(step, jnp.zeros((b, h, dk, dv), f32),
                        (jnp.moveaxis(q_decayed, 2, 0), jnp.moveaxis(chunk_kv, 2, 0)))
    out = intra_out + jnp.moveaxis(inter, 0, 2)
    return out.transpose(0, 2, 3, 1, 4).reshape(b, s, h, dv)


def hybrid_mixer(u, w_in, attn_q_norm, attn_k_norm, attn_sinks, dn_conv, dn_a_log, dn_dt_bias,
                 dn_out_norm, w_branch, w_out):
    b, s, _ = u.shape
    f32 = jnp.float32
    proj = u @ w_in
    (a_q, a_k, a_v, d_qkv, d_a, d_b, d_z, r_q, r_k, r_v, r_g, gate_logits) = _split_columns(proj, IN_SIZES)
    pos = jnp.arange(s, dtype=f32)

    q = a_q.reshape(b, s, ATTN_Q_HEADS, ATTN_HEAD_DIM)
    k = a_k.reshape(b, s, ATTN_KV_HEADS, ATTN_HEAD_DIM)
    v = a_v.reshape(b, s, ATTN_KV_HEADS, ATTN_HEAD_DIM)
    q = partial_rope(rms_norm(q) * attn_q_norm, pos)
    k = partial_rope(rms_norm(k) * attn_k_norm, pos)
    out_a = sliding_window_attention(q, k, v, attn_sinks)

    qkv = jax.nn.silu(causal_depthwise_conv(d_qkv, dn_conv))
    dq, dk_, dv_ = _split_columns(qkv, (DN_HEADS * DN_KEY_DIM, DN_HEADS * DN_KEY_DIM, DN_HEADS * DN_VALUE_DIM))
    dq = l2_norm(dq.reshape(b, s, DN_HEADS, DN_KEY_DIM))
    dk_ = l2_norm(dk_.reshape(b, s, DN_HEADS, DN_KEY_DIM))
    dv_ = dv_.reshape(b, s, DN_HEADS, DN_VALUE_DIM)
    log_decay = -jnp.exp(dn_a_log.astype(f32)) * jax.nn.softplus(d_a.astype(f32) + dn_dt_bias.astype(f32))
    beta = jax.nn.sigmoid(d_b.astype(f32))
    o_b = gated_delta_rule_chunked(dq, dk_, dv_, log_decay, beta).astype(u.dtype)
    o_b = rms_norm(o_b) * dn_out_norm * jax.nn.silu(d_z.reshape(b, s, DN_HEADS, DN_VALUE_DIM))
    out_b = o_b.reshape(b, s, DN_HEADS * DN_VALUE_DIM)

    rq = retnet_rotate(r_q.reshape(b, s, RET_HEADS, RET_KEY_DIM), pos)
    rk = retnet_rotate(r_k.reshape(b, s, RET_HEADS, RET_KEY_DIM), pos) * (RET_KEY_DIM ** -0.5)
    rv = r_v.reshape(b, s, RET_HEADS, RET_VALUE_DIM)
    o_c = rms_norm(retention_chunked(rq, rk, rv).astype(u.dtype))
    out_c = o_c.reshape(b, s, RET_HEADS * RET_VALUE_DIM) * jax.nn.silu(r_g)

    branches = jnp.stack([out_a, out_b, out_c], axis=2)
    per_branch = jnp.einsum('bsgi,gid->bsgd', branches, w_branch)
    gates = jax.nn.sigmoid(gate_logits.reshape(b, s, N_BRANCH, D_MODEL))
    merged = jnp.sum(gates * per_branch, axis=2)
    return merged @ w_out


def setup_inputs(seed: int = 0) -> dict:
    key = jax.random.key(seed)
    ks = jax.random.split(key, 24)
    f32 = jnp.float32
    L, D = DEPTH, D_MODEL

    def normal(k, shape, scale):
        return jax.random.normal(k, shape, f32) * scale

    def gain(k, shape):
        return 1.0 + 0.1 * jax.random.normal(k, shape, f32)

    dt = jnp.exp(jax.random.uniform(ks[13], (L, DN_HEADS), f32, minval=math.log(1e-3), maxval=math.log(1e-1)))
    return {
        'x': normal(ks[0], (BATCH, SEQ, D), 1.0),
        'c': normal(ks[1], (BATCH, D), 1.0),
        'w_mod': normal(ks[2], (L, D, N_MOD * D), 0.5 * D ** -0.5),
        'b_mod': normal(ks[3], (L, N_MOD * D), 0.02),
        'ffn1_norm': gain(ks[4], (L, D)),
        'ffn1_w13': normal(ks[5], (L, D, 2 * D_FF), D ** -0.5),
        'ffn1_w2': normal(ks[6], (L, D_FF, D), D_FF ** -0.5),
        'mix_norm': gain(ks[7], (L, D)),
        'w_in': normal(ks[8], (L, D, N_IN), D ** -0.5),
        'attn_q_norm': gain(ks[9], (L, ATTN_HEAD_DIM)),
        'attn_k_norm': gain(ks[10], (L, ATTN_HEAD_DIM)),
        'attn_sinks': normal(ks[11], (L, ATTN_Q_HEADS), 1.0),
        'dn_conv': normal(ks[12], (L, DN_CONV, DN_QKV_WIDTH), DN_CONV ** -0.5),
        'dn_a_log': jnp.log(jax.random.uniform(ks[14], (L, DN_HEADS), f32, minval=1.0, maxval=16.0)),
        'dn_dt_bias': jnp.log(jnp.expm1(dt)),
        'dn_out_norm': gain(ks[15], (L, DN_VALUE_DIM)),
        'w_branch': normal(ks[16], (L, N_BRANCH, BRANCH_WIDTH, D), BRANCH_WIDTH ** -0.5),
        'w_out': normal(ks[17], (L, D, D), D ** -0.5),
        'ffn2_norm': gain(ks[18], (L, D)),
        'ffn2_w13': normal(ks[19], (L, D, 2 * D_FF), D ** -0.5),
        'ffn2_w2': normal(ks[20], (L, D_FF, D), D_FF ** -0.5),
    }


def reference(x, c, w_mod, b_mod, ffn1_norm, ffn1_w13, ffn1_w2, mix_norm, w_in, attn_q_norm,
              attn_k_norm, attn_sinks, dn_conv, dn_a_log, dn_dt_bias, dn_out_norm, w_branch, w_out,
              ffn2_norm, ffn2_w13, ffn2_w2):
    b = x.shape[0]
    cond = jax.nn.silu(c)
    for layer in range(DEPTH):
        mod = (cond @ w_mod[layer] + b_mod[layer]).reshape(b, N_MOD, 1, D_MODEL)
        u1 = modulate(x, ffn1_norm[layer], mod[:, 0], mod[:, 1])
        x = x + 0.5 * mod[:, 2] * swiglu_ffn(u1, ffn1_w13[layer], ffn1_w2[layer])
        u2 = modulate(x, mix_norm[layer], mod[:, 3], mod[:, 4])
        x = x + mod[:, 5] * hybrid_mixer(u2, w_in[layer], attn_q_norm[layer], attn_k_norm[layer],
                                         attn_sinks[layer], dn_conv[layer], dn_a_log[layer],
                                         dn_dt_bias[layer], dn_out_norm[layer], w_branch[layer],
                                         w_out[layer])
        u3 = modulate(x, ffn2_norm[layer], mod[:, 6], mod[:, 7])
        x = x + 0.5 * mod[:, 8] * swiglu_ffn(u3, ffn2_w13[layer], ffn2_w2[layer])
    return x
```

```python
import functools
import math

import numpy as np
import jax
import jax.numpy as jnp
from jax import lax
from jax.experimental import pallas as pl
from jax.experimental.pallas import tpu as pltpu

F32 = jnp.float32
BF16 = jnp.bfloat16
HIGHEST = lax.Precision.HIGHEST

D_MODEL = 1024
BATCH = 8
SEQ = 2048
DEPTH = 2
N_TOK = BATCH * SEQ
N_MOD = 9
D_FF = 2816
EPS = 1e-6
NEG_INF = -1e30

ATTN_Q_HEADS = 8
ATTN_HEAD_DIM = 64
ATTN_BLOCK = 128
ROPE_DIM = 16
ROPE_THETA = 500000.0
DN_HEADS = 8
DN_DIM = 64
DN_CONV = 4
RET_HEADS = 4
RET_KEY_DIM = 64
RET_THETA = 10000.0

LANES = 128
SUBLANES = 8

P_GATES = 0
P_ATTN = 3072
P_Z = 4096
P_DQKV = 4608
P_RET = 6144
N_PROJ = 7680
ATTN_W = 1024
ATTN_QK = 768

TM = 512
FFN_CHUNK = 256
N_FFN_CHUNK = D_FF // FFN_CHUNK
PROJ_CHUNK = 768
MOD_TN = 1152
SEQ_TILE = 256
DN_CHUNK = 64
VMEM_LIMIT = 56 * 1024 * 1024


def _dot(a, b, precision=None):
    return jnp.dot(a, b, preferred_element_type=F32, precision=precision)


def _dot_nt(a, b):
    return lax.dot_general(a, b, (((1,), (1,)), ((), ())), preferred_element_type=F32)


def _dot_tn(a, b):
    return lax.dot_general(a, b, (((0,), (0,)), ((), ())), preferred_element_type=F32)


def _split_bf16(a):
    hi = a.astype(BF16)
    return hi, (a - hi.astype(F32)).astype(BF16)


def _dot_inv(a, b):
    ah, al = _split_bf16(a)
    bh, bl = _split_bf16(b)
    return _dot(ah, bh) + (_dot(ah, bl) + _dot(al, bh))


def _sigmoid(x):
    return 1.0 / (1.0 + jnp.exp(-x))


def _silu(x):
    return x * _sigmoid(x)


def _modulate(x, gain, shift, scale):
    ms = jnp.mean(x * x, axis=-1, keepdims=True)
    return x * lax.rsqrt(ms + EPS) * (gain * (1.0 + scale)) + shift


def _seg_sum(x2, bd):
    hi = x2.astype(BF16)
    lo = (x2 - hi.astype(F32)).astype(BF16)
    return _dot(hi, bd) + _dot(lo, bd)


def _roll_lanes(x, shift):
    n = x.shape[-1]
    parts = [pltpu.roll(x[:, i:i + LANES], shift, 1) for i in range(0, n, LANES)]
    return parts[0] if len(parts) == 1 else jnp.concatenate(parts, axis=1)


def _rotate(x, c, a, b, sh):
    return x * c + _roll_lanes(x, LANES - sh) * a + _roll_lanes(x, sh) * b


def _cparams(*sem):
    return pltpu.CompilerParams(dimension_semantics=sem, vmem_limit_bytes=VMEM_LIMIT)


def _resident(shape):
    zeros = (0,) * len(shape)
    return pl.BlockSpec(shape, lambda *_: zeros, pipeline_mode=pl.Buffered(1))


def _mod_kernel(c_ref, w_ref, b_ref, o_ref):
    c = c_ref[...]
    o_ref[0] = _dot(_silu(c), w_ref[0], precision=HIGHEST) + b_ref[0]


def _mod_call(c, w_mod, b_mod):
    n = N_MOD * D_MODEL
    return pl.pallas_call(
        _mod_kernel,
        grid=(DEPTH, n // MOD_TN),
        in_specs=[
            pl.BlockSpec((BATCH, D_MODEL), lambda l, j: (0, 0)),
            pl.BlockSpec((1, D_MODEL, MOD_TN), lambda l, j: (l, 0, j)),
            pl.BlockSpec((1, 1, MOD_TN), lambda l, j: (l, 0, j)),
        ],
        out_specs=pl.BlockSpec((1, BATCH, MOD_TN), lambda l, j: (l, 0, j)),
        out_shape=jax.ShapeDtypeStruct((DEPTH, BATCH, n), F32),
        compiler_params=_cparams("parallel", "parallel"),
        name="adaln_mod",
    )(c, w_mod, b_mod.reshape(DEPTH, 1, n))


def _ffn_kernel(x_ref, mod_ref, gain_ref, w13_ref, w2_ref, o_ref, h_ref):
    x = x_ref[...]
    mod = mod_ref[0]
    ub = _modulate(x, gain_ref[...], mod[0:1], mod[1:2]).astype(BF16)
    for j in range(N_FFN_CHUNK):
        gu = _dot(ub, w13_ref[j])
        g = gu[:, :FFN_CHUNK]
        up = gu[:, FFN_CHUNK:]
        h_ref[:, j * FFN_CHUNK:(j + 1) * FFN_CHUNK] = (_silu(g) * up).astype(BF16)
    y = _dot(h_ref[...], w2_ref[...])
    o_ref[...] = x + (0.5 * mod[2:3]) * y


def _ffn_call(x2d, mod3, gain, w13r, w2):
    tiles_per_batch = SEQ // TM
    return pl.pallas_call(
        _ffn_kernel,
        grid=(N_TOK // TM,),
        in_specs=[
            pl.BlockSpec((TM, D_MODEL), lambda i: (i, 0)),
            pl.BlockSpec((1, 3, D_MODEL), lambda i: (i // tiles_per_batch, 0, 0)),
            _resident((1, D_MODEL)),
            _resident((N_FFN_CHUNK, D_MODEL, 2 * FFN_CHUNK)),
            _resident((D_FF, D_MODEL)),
        ],
        out_specs=pl.BlockSpec((TM, D_MODEL), lambda i: (i, 0)),
        out_shape=jax.ShapeDtypeStruct((N_TOK, D_MODEL), F32),
        scratch_shapes=[pltpu.VMEM((TM, D_FF), BF16)],
        compiler_params=_cparams("parallel"),
        name="swiglu_ffn",
    )(x2d, mod3, gain, w13r, w2)


def _inproj_kernel(x_ref, mod_ref, gain_ref, w_ref, wab_ref, wabt_ref, proj_ref, ab_ref, abt_ref):
    mod = mod_ref[0]
    ub = _modulate(x_ref[...], gain_ref[...], mod[0:1], mod[1:2]).astype(BF16)
    for j in range(N_PROJ // PROJ_CHUNK):
        sl = slice(j * PROJ_CHUNK, (j + 1) * PROJ_CHUNK)
        proj_ref[:, sl] = _dot(ub, w_ref[:, sl]).astype(BF16)
    ab_ref[...] = _dot(ub, wab_ref[...])
    abt_ref[0] = _dot_nt(wabt_ref[...], ub)


def _inproj_call(x2d, mod3, gain, w_in_p, w_ab, w_abt):
    tiles_per_batch = SEQ // TM
    return pl.pallas_call(
        _inproj_kernel,
        grid=(N_TOK // TM,),
        in_specs=[
            pl.BlockSpec((TM, D_MODEL), lambda i: (i, 0)),
            pl.BlockSpec((1, 3, D_MODEL), lambda i: (i // tiles_per_batch, 0, 0)),
            _resident((1, D_MODEL)),
            _resident((D_MODEL, N_PROJ)),
            _resident((D_MODEL, LANES)),
            _resident((2 * DN_HEADS, D_MODEL)),
        ],
        out_specs=[
            pl.BlockSpec((TM, N_PROJ), lambda i: (i, 0)),
            pl.BlockSpec((TM, LANES), lambda i: (i, 0)),
            pl.BlockSpec((1, 2 * DN_HEADS, TM), lambda i: (i // tiles_per_batch, 0, i % tiles_per_batch)),
        ],
        out_shape=[
            jax.ShapeDtypeStruct((N_TOK, N_PROJ), BF16),
            jax.ShapeDtypeStruct((N_TOK, LANES), F32),
            jax.ShapeDtypeStruct((BATCH, 2 * DN_HEADS, SEQ), F32),
        ],
        compiler_params=_cparams("parallel"),
        name="mixer_in_proj",
    )(x2d, mod3, gain, w_in_p, w_ab, w_abt)


def _head_norm(x, bd, gain):
    s = _seg_sum(x * x, bd)
    return x * lax.rsqrt(s * (1.0 / ATTN_HEAD_DIM) + EPS) * gain


def _attn_kernel(cur_ref, prev_ref, tc_ref, ta_ref, tb_ref, pc_ref, pa_ref, pb_ref,
                 gain_ref, bd_ref, sink_ref, o_ref):
    n = pl.program_id(1)
    blk = ATTN_BLOCK
    cur = cur_ref[0]
    prv = prev_ref[0]
    gain = gain_ref[...]
    kw = ATTN_QK - 512
    qk = _rotate(_head_norm(cur[:, :ATTN_QK].astype(F32), bd_ref[...], gain),
                 tc_ref[...], ta_ref[...], tb_ref[...], ROPE_DIM // 2)
    kprev = _rotate(_head_norm(prv[:, 512:ATTN_QK].astype(F32), bd_ref[0:kw, 0:kw], gain[:, 512:]),
                    pc_ref[...], pa_ref[...], pb_ref[...], ROPE_DIM // 2)
    kband = jnp.concatenate([kprev, qk[:, 512:]], axis=0).astype(BF16)
    vband = jnp.concatenate([prv[:, ATTN_QK:], cur[:, ATTN_QK:]], axis=0)

    qi = lax.broadcasted_iota(jnp.int32, (blk, 2 * blk), 0)
    kj = lax.broadcasted_iota(jnp.int32, (blk, 2 * blk), 1)
    valid = (kj > qi) & (kj <= qi + blk) & ((kj >= blk) | (n > 0))
    low_half = lax.broadcasted_iota(jnp.int32, (1, LANES), 1) < ATTN_HEAD_DIM
    sinks = sink_ref[...]

    for p in range(ATTN_Q_HEADS // 2):
        qp = qk[:, p * LANES:(p + 1) * LANES]
        grp = (2 * p) // 4
        kg = kband[:, grp * LANES:(grp + 1) * LANES]
        vg = vband[:, grp * LANES:(grp + 1) * LANES]
        outs = []
        for half in range(2):
            h = 2 * p + half
            mh = low_half if half == 0 else jnp.logical_not(low_half)
            qh = jnp.where(mh, qp, 0.0).astype(BF16)
            logits = _dot_nt(qh, kg) * (ATTN_HEAD_DIM ** -0.5)
            logits = jnp.where(valid, logits, NEG_INF)
            sink = sinks[h:h + 1, 0:1]
            m = jnp.maximum(jnp.max(logits, axis=1, keepdims=True), sink)
            e = jnp.exp(logits - m)
            den = jnp.sum(e, axis=1, keepdims=True) + jnp.exp(sink - m)
            probs = (e / den).astype(BF16)
            outs.append(_dot(probs, vg))
        o_ref[0, :, p * LANES:(p + 1) * LANES] = jnp.where(low_half, outs[0], outs[1]).astype(BF16)


def _attn_call(proj3, tabs, gain, bd, sinks):
    nb = SEQ // ATTN_BLOCK
    col = P_ATTN // ATTN_W
    prev = lambda n: jnp.maximum(n - 1, 0)
    tab_cur = pl.BlockSpec((ATTN_BLOCK, ATTN_QK), lambda b, n: (n, 0))
    tab_prev = pl.BlockSpec((ATTN_BLOCK, ATTN_QK - 512), lambda b, n: (prev(n), 2))
    return pl.pallas_call(
        _attn_kernel,
        grid=(BATCH, nb),
        in_specs=[
            pl.BlockSpec((1, ATTN_BLOCK, ATTN_W), lambda b, n: (b, n, col)),
            pl.BlockSpec((1, ATTN_BLOCK, ATTN_W), lambda b, n: (b, prev(n), col)),
            tab_cur, tab_cur, tab_cur, tab_prev, tab_prev, tab_prev,
            _resident((1, ATTN_QK)),
            _resident((ATTN_QK, ATTN_QK)),
            _resident((ATTN_Q_HEADS, LANES)),
        ],
        out_specs=pl.BlockSpec((1, ATTN_BLOCK, 512), lambda b, n: (b, n, 0)),
        out_shape=jax.ShapeDtypeStruct((BATCH, SEQ, 512), BF16),
        compiler_params=_cparams("parallel", "parallel"),
        name="swa_attention",
    )(proj3, proj3, tabs[0], tabs[1], tabs[2], tabs[0], tabs[1], tabs[2], gain, bd, sinks)


def _softplus(x):
    return jnp.maximum(x, 0.0) + jnp.log(1.0 + jnp.exp(-jnp.abs(x)))


def _dn_kernel(cur_ref, prev_ref, z_ref, ab_ref, abt_ref, conv_ref, alog_ref, dtb_ref,
               alogt_ref, dtbt_ref, onorm_ref, bd_ref, ea_ref, eb_ref, o_ref, state_ref):
    t = pl.program_id(1)
    T = SEQ_TILE
    C = DN_CHUNK
    nh = DN_HEADS
    width = nh * DN_DIM

    @pl.when(t == 0)
    def _():
        state_ref[...] = jnp.zeros_like(state_ref)

    x = cur_ref[0].astype(F32)
    tail = jnp.where(t > 0, prev_ref[0].astype(F32), 0.0)
    xc = jnp.concatenate([tail, x], axis=0)
    w = conv_ref[...]
    y = xc[SUBLANES:SUBLANES + T] * w[3:4]
    for j in range(DN_CONV - 1):
        off = SUBLANES - (DN_CONV - 1) + j
        y = y + xc[off:off + T] * w[j:j + 1]
    y = _silu(y)
    bd = bd_ref[...]
    q = y[:, :width]
    k = y[:, width:2 * width]
    v = y[:, 2 * width:]
    q = q * lax.rsqrt(_seg_sum(q * q, bd) + EPS) * (DN_DIM ** -0.5)
    k = k * lax.rsqrt(_seg_sum(k * k, bd) + EPS)

    ab = ab_ref[...]
    la = -jnp.exp(alog_ref[...]) * _softplus(ab + dtb_ref[...])
    beta = _sigmoid(ab)
    lat = -jnp.exp(alogt_ref[...]) * _softplus(abt_ref[0] + dtbt_ref[...])
    ri = lax.broadcasted_iota(jnp.int32, (T, T), 0)
    ci = lax.broadcasted_iota(jnp.int32, (T, T), 1)
    same = (ri // C) == (ci // C)
    causal = same & (ci <= ri)
    strict = same & (ci < ri)
    g_nat = _dot(causal.astype(F32), la, precision=HIGHEST)
    gend_nat = _dot(same.astype(F32), la, precision=HIGHEST)
    g_t = _dot(lat, (same & (ri <= ci)).astype(F32), precision=HIGHEST)
    gx = _dot(g_nat, ea_ref[...], precision=HIGHEST)
    gendx = _dot(gend_nat, ea_ref[...], precision=HIGHEST)
    bx = _dot(beta, eb_ref[...], precision=HIGHEST)
    eg = jnp.exp(gx)
    kb = k * bx
    vb = v * bx
    kbe = kb * eg
    qd = q * eg
    kt = k * jnp.exp(gendx - gx)
    dec_end = jnp.exp(gendx)

    low_half = lax.broadcasted_iota(jnp.int32, (1, LANES), 1) < DN_DIM
    bdr = lax.broadcasted_iota(jnp.int32, (LANES, LANES), 0) // DN_DIM
    bdc = lax.broadcasted_iota(jnp.int32, (LANES, LANES), 1) // DN_DIM
    blockdiag = bdr == bdc

    for p in range(nh // 2):
        sl = slice(p * LANES, (p + 1) * LANES)
        k_p = k[:, sl].astype(BF16)
        u_p = jnp.zeros((T, LANES), F32)
        w_p = jnp.zeros((T, LANES), F32)
        intras = []
        for half in range(2):
            h = 2 * p + half
            mh = low_half if half == 0 else jnp.logical_not(low_half)
            diff = g_nat[:, h:h + 1] - g_t[h:h + 1, :]
            decay = jnp.exp(jnp.where(causal, diff, NEG_INF))
            kk = _dot_nt(jnp.where(mh, kb[:, sl], 0.0).astype(BF16), k_p)
            qkm = _dot_nt(jnp.where(mh, q[:, sl], 0.0).astype(BF16), k_p)
            intras.append((qkm * decay).astype(BF16))
            pw = jnp.where(strict, -(kk * decay), 0.0)
            yv = jnp.concatenate([jnp.where(mh, vb[:, sl], 0.0), jnp.where(mh, kbe[:, sl], 0.0)], axis=1)
            for lvl in range(6):
                yv = yv + _dot_inv(pw, yv)
                if lvl < 5:
                    pw = _dot_inv(pw, pw)
            u_p = u_p + yv[:, :LANES]
            w_p = w_p + yv[:, LANES:]
        state = state_ref[p]
        vn, oi = [], []
        for c in range(T // C):
            rows = slice(c * C, (c + 1) * C)
            sb = state.astype(BF16)
            vn_c = u_p[rows] - _dot(w_p[rows].astype(BF16), sb)
            oi.append(_dot(qd[rows, sl].astype(BF16), sb))
            upd = _dot_tn(kt[rows, sl].astype(BF16), vn_c.astype(BF16))
            state = state * dec_end[c * C:c * C + 1, sl] + jnp.where(blockdiag, upd, 0.0)
            vn.append(vn_c)
        state_ref[p] = state
        vn_all = jnp.concatenate(vn, axis=0)
        o_p = jnp.concatenate(oi, axis=0)
        for half in range(2):
            mh = low_half if half == 0 else jnp.logical_not(low_half)
            o_p = o_p + _dot(intras[half], jnp.where(mh, vn_all, 0.0).astype(BF16))
        ms = _seg_sum(o_p * o_p, bd[0:LANES, 0:LANES]) * (1.0 / DN_DIM)
        zg = _silu(z_ref[0, :, sl].astype(F32))
        o_ref[0, :, sl] = (o_p * lax.rsqrt(ms + EPS) * onorm_ref[:, sl] * zg).astype(BF16)


def _dn_call(proj3, ab, abt, conv_w, alog, dtb, alog_t, dtb_t, onorm, bd, ea, eb):
    T = SEQ_TILE
    rows8 = T // SUBLANES
    return pl.pallas_call(
        _dn_kernel,
        grid=(BATCH, SEQ // T),
        in_specs=[
            pl.BlockSpec((1, T, 1536), lambda b, t: (b, t, P_DQKV // 1536)),
            pl.BlockSpec((1, SUBLANES, 1536), lambda b, t: (b, jnp.maximum(t * rows8 - 1, 0), P_DQKV // 1536)),
            pl.BlockSpec((1, T, 512), lambda b, t: (b, t, P_Z // 512)),
            pl.BlockSpec((T, LANES), lambda b, t: (b * (SEQ // T) + t, 0)),
            pl.BlockSpec((1, 2 * DN_HEADS, T), lambda b, t: (b, 0, t)),
            _resident((DN_CONV, 1536)),
            _resident((1, LANES)),
            _resident((1, LANES)),
            _resident((2 * DN_HEADS, T)),
            _resident((2 * DN_HEADS, T)),
            _resident((1, 512)),
            _resident((512, 512)),
            _resident((LANES, 512)),
            _resident((LANES, 512)),
        ],
        out_specs=pl.BlockSpec((1, T, 512), lambda b, t: (b, t, 0)),
        out_shape=jax.ShapeDtypeStruct((BATCH, SEQ, 512), BF16),
        scratch_shapes=[pltpu.VMEM((DN_HEADS // 2, LANES, LANES), F32)],
        compiler_params=_cparams("parallel", "arbitrary"),
        name="gated_deltanet",
    )(proj3, proj3, proj3, ab, abt, conv_w, alog, dtb, alog_t, dtb_t, onorm, bd, ea, eb)


def _ret_kernel(x_ref, tc_ref, ta_ref, tb_ref, o_ref, state_ref):
    t = pl.program_id(1)
    T = SEQ_TILE

    @pl.when(t == 0)
    def _():
        state_ref[...] = jnp.zeros_like(state_ref)

    x = x_ref[0]
    qk = _rotate(x[:, :512].astype(F32), tc_ref[...], ta_ref[...], tb_ref[...], 1)
    rq = qk[:, :256]
    rk = qk[:, 256:] * (RET_KEY_DIM ** -0.5)
    ri = lax.broadcasted_iota(jnp.int32, (T, T), 0)
    ci = lax.broadcasted_iota(jnp.int32, (T, T), 1)
    causal = ci <= ri
    dist = (ri - ci).astype(F32)
    row = lax.broadcasted_iota(jnp.int32, (T, LANES), 0).astype(F32)
    low_half = lax.broadcasted_iota(jnp.int32, (1, LANES), 1) < RET_KEY_DIM

    for h in range(RET_HEADS):
        log_gamma = math.log1p(-(2.0 ** (-5.0 - h)))
        p = h // 2
        mh = low_half if h % 2 == 0 else jnp.logical_not(low_half)
        q_p = rq[:, p * LANES:(p + 1) * LANES]
        k_p = rk[:, p * LANES:(p + 1) * LANES]
        v_h = x[:, 512 + h * LANES:512 + (h + 1) * LANES]
        decay = jnp.exp(jnp.where(causal, dist * log_gamma, NEG_INF))
        scores = _dot_nt(jnp.where(mh, q_p, 0.0).astype(BF16), k_p.astype(BF16)) * decay
        state = state_ref[h]
        q_dec = jnp.where(mh, q_p * jnp.exp(log_gamma * (row + 1.0)), 0.0)
        out = _dot(scores.astype(BF16), v_h) + _dot(q_dec.astype(BF16), state.astype(BF16))
        k_end = jnp.where(mh, k_p * jnp.exp(log_gamma * (T - 1.0 - row)), 0.0)
        state_ref[h] = state * math.exp(log_gamma * T) + _dot_tn(k_end.astype(BF16), v_h)
        ms = jnp.mean(out * out, axis=-1, keepdims=True)
        g = _silu(x[:, 1024 + h * LANES:1024 + (h + 1) * LANES].astype(F32))
        o_ref[0, :, h * LANES:(h + 1) * LANES] = (out * lax.rsqrt(ms + EPS) * g).astype(BF16)


def _ret_call(proj3, tabs):
    T = SEQ_TILE
    tab = pl.BlockSpec((T, 512), lambda b, t: (t, 0))
    return pl.pallas_call(
        _ret_kernel,
        grid=(BATCH, SEQ // T),
        in_specs=[pl.BlockSpec((1, T, 1536), lambda b, t: (b, t, P_RET // 1536)), tab, tab, tab],
        out_specs=pl.BlockSpec((1, T, 512), lambda b, t: (b, t, 0)),
        out_shape=jax.ShapeDtypeStruct((BATCH, SEQ, 512), BF16),
        scratch_shapes=[pltpu.VMEM((RET_HEADS, LANES, LANES), F32)],
        compiler_params=_cparams("parallel", "arbitrary"),
        name="retention",
    )(proj3, tabs[0], tabs[1], tabs[2])


def _merge_kernel(x_ref, mod_ref, a_ref, b_ref, c_ref, g_ref, wb_ref, wo_ref, o_ref):
    merged = jnp.zeros((TM, D_MODEL), F32)
    for i, br in enumerate((a_ref, b_ref, c_ref)):
        gate = _sigmoid(g_ref[:, i * D_MODEL:(i + 1) * D_MODEL].astype(F32))
        merged = merged + gate * _dot(br[...], wb_ref[i])
    y = _dot(merged.astype(BF16), wo_ref[...])
    o_ref[...] = x_ref[...] + mod_ref[0][2:3] * y


def _merge_call(x2d, mod3, out_a, out_b, out_c, proj2, w_branch, w_out):
    tiles_per_batch = SEQ // TM
    branch = pl.BlockSpec((TM, 512), lambda i: (i, 0))
    return pl.pallas_call(
        _merge_kernel,
        grid=(N_TOK // TM,),
        in_specs=[
            pl.BlockSpec((TM, D_MODEL), lambda i: (i, 0)),
            pl.BlockSpec((1, 3, D_MODEL), lambda i: (i // tiles_per_batch, 0, 0)),
            branch, branch, branch,
            pl.BlockSpec((TM, 3 * D_MODEL), lambda i: (i, P_GATES // (3 * D_MODEL))),
            _resident((3, 512, D_MODEL)),
            _resident((D_MODEL, D_MODEL)),
        ],
        out_specs=pl.BlockSpec((TM, D_MODEL), lambda i: (i, 0)),
        out_shape=jax.ShapeDtypeStruct((N_TOK, D_MODEL), F32),
        compiler_params=_cparams("parallel"),
        name="branch_merge",
    )(x2d, mod3, out_a, out_b, out_c, proj2, w_branch, w_out)


def _block_diag_ones(n, seg):
    idx = np.arange(n) // seg
    return jnp.asarray((idx[:, None] == idx[None, :]).astype(np.float32), dtype=BF16)


def _head_expander(first_lane):
    e = np.zeros((LANES, DN_HEADS * DN_DIM), np.float32)
    for h in range(DN_HEADS):
        e[first_lane + h, h * DN_DIM:(h + 1) * DN_DIM] = 1.0
    return jnp.asarray(e)


def _attn_tables():
    half = ROPE_DIM // 2
    pos = jnp.arange(SEQ, dtype=F32)
    inv_freq = ROPE_THETA ** (-jnp.arange(0, ROPE_DIM, 2, dtype=F32) / ROPE_DIM)
    phase = pos[:, None] * inv_freq[None, :]
    cos, sin = jnp.cos(phase), jnp.sin(phase)
    pad = ATTN_HEAD_DIM - ROPE_DIM
    c = jnp.concatenate([cos, cos, jnp.ones((SEQ, pad), F32)], axis=1)
    a = jnp.concatenate([-sin, jnp.zeros((SEQ, ATTN_HEAD_DIM - half), F32)], axis=1)
    b = jnp.concatenate([jnp.zeros((SEQ, half), F32), sin, jnp.zeros((SEQ, pad), F32)], axis=1)
    reps = ATTN_QK // ATTN_HEAD_DIM
    return tuple(jnp.tile(t, (1, reps)) for t in (c, a, b))


def _ret_tables():
    pos = jnp.arange(SEQ, dtype=F32)
    angle = 1.0 / (RET_THETA ** jnp.linspace(0.0, 1.0, RET_KEY_DIM // 2, dtype=F32))
    angle = jnp.repeat(angle, 2)
    phase = pos[:, None] * angle[None, :]
    cos, sin = jnp.cos(phase), jnp.sin(phase)
    even = (jnp.arange(RET_KEY_DIM) % 2 == 0)[None, :]
    a = jnp.where(even, -sin, 0.0)
    b = jnp.where(even, 0.0, sin)
    reps = 512 // RET_KEY_DIM
    return tuple(jnp.tile(t, (1, reps)) for t in (cos, a, b))


def _relayout_w_in(w):
    cs = lambda a, b: w[:, a:b]
    k0, k1 = cs(512, 576), cs(576, 640)
    v0, v1 = cs(640, 704), cs(704, 768)
    cols = [cs(4368, 7440), cs(0, 512), k0, k0, k1, k1, v0, v0, v1, v1,
            cs(2320, 2832), cs(768, 2304), cs(2832, 3344), cs(3344, 3856), cs(3856, 4368)]
    w_p = jnp.concatenate(cols, axis=1).astype(BF16)
    ab = cs(2304, 2320)
    w_ab = jnp.concatenate([ab, jnp.zeros((D_MODEL, LANES - 2 * DN_HEADS), F32)], axis=1).astype(BF16)
    return w_p, w_ab, ab.T.astype(BF16)


def _relayout_w13(w13):
    g = w13[:, :D_FF].reshape(D_MODEL, N_FFN_CHUNK, FFN_CHUNK)
    u = w13[:, D_FF:].reshape(D_MODEL, N_FFN_CHUNK, FFN_CHUNK)
    return jnp.concatenate([g, u], axis=2).transpose(1, 0, 2).astype(BF16)


def _lane_row(vec, first_lane=0):
    row = jnp.zeros((1, LANES), F32)
    return row.at[0, first_lane:first_lane + vec.shape[0]].set(vec)


def _rows_t(vec):
    col = jnp.concatenate([vec, jnp.zeros((2 * DN_HEADS - vec.shape[0],), F32)])
    return jnp.broadcast_to(col[:, None], (2 * DN_HEADS, SEQ_TILE))


def kernel(x, c, w_mod, b_mod, ffn1_norm, ffn1_w13, ffn1_w2, mix_norm, w_in, attn_q_norm, attn_k_norm,
           attn_sinks, dn_conv, dn_a_log, dn_dt_bias, dn_out_norm, w_branch, w_out, ffn2_norm, ffn2_w13,
           ffn2_w2):
    mod = _mod_call(c, w_mod, b_mod).reshape(DEPTH, BATCH, N_MOD, D_MODEL)
    attn_tabs = _attn_tables()
    ret_tabs = _ret_tables()
    bd_attn = _block_diag_ones(ATTN_QK, ATTN_HEAD_DIM)
    bd_dn = _block_diag_ones(DN_HEADS * DN_DIM, DN_DIM)
    ea = _head_expander(0)
    eb = _head_expander(DN_HEADS)

    xf = x.reshape(N_TOK, D_MODEL)
    for l in range(DEPTH):
        xf = _ffn_call(xf, mod[l, :, 0:3], ffn1_norm[l][None], _relayout_w13(ffn1_w13[l]),
                       ffn1_w2[l].astype(BF16))

        w_in_p, w_ab, w_abt = _relayout_w_in(w_in[l])
        proj, ab, abt = _inproj_call(xf, mod[l, :, 3:6], mix_norm[l][None], w_in_p, w_ab, w_abt)
        proj3 = proj.reshape(BATCH, SEQ, N_PROJ)
        qk_gain = jnp.concatenate([jnp.tile(attn_q_norm[l], ATTN_Q_HEADS),
                                   jnp.tile(attn_k_norm[l], (ATTN_QK - 512) // ATTN_HEAD_DIM)])[None]
        sinks = jnp.broadcast_to(attn_sinks[l][:, None], (ATTN_Q_HEADS, LANES))
        out_a = _attn_call(proj3, attn_tabs, qk_gain, bd_attn, sinks)
        out_b = _dn_call(proj3, ab, abt, dn_conv[l], _lane_row(dn_a_log[l]), _lane_row(dn_dt_bias[l]),
                         _rows_t(dn_a_log[l]), _rows_t(dn_dt_bias[l]),
                         jnp.tile(dn_out_norm[l], DN_HEADS)[None], bd_dn, ea, eb)
        out_c = _ret_call(proj3, ret_tabs)
        xf = _merge_call(xf, mod[l, :, 3:6], out_a.reshape(N_TOK, 512), out_b.reshape(N_TOK, 512),
                         out_c.reshape(N_TOK, 512), proj, w_branch[l].astype(BF16), w_out[l].astype(BF16))

        xf = _ffn_call(xf, mod[l, :, 6:9], ffn2_norm[l][None], _relayout_w13(ffn2_w13[l]),
                       ffn2_w2[l].astype(BF16))
    return xf.reshape(BATCH, SEQ, D_MODEL)
```

```python
import functools
import math

import numpy as np
import jax
import jax.numpy as jnp
from jax import lax
from jax.experimental import pallas as pl
from jax.experimental.pallas import tpu as pltpu

F32 = jnp.float32
BF16 = jnp.bfloat16
HIGHEST = lax.Precision.HIGHEST

D_MODEL = 1024
BATCH = 8
SEQ = 2048
DEPTH = 2
N_TOK = BATCH * SEQ
N_MOD = 9
D_FF = 2816
EPS = 1e-6
NEG_INF = -1e30

ATTN_Q_HEADS = 8
ATTN_HEAD_DIM = 64
ATTN_BLOCK = 128
ROPE_DIM = 16
ROPE_THETA = 500000.0
DN_HEADS = 8
DN_DIM = 64
DN_CONV = 4
RET_HEADS = 4
RET_KEY_DIM = 64
RET_THETA = 10000.0

LANES = 128
SUBLANES = 8

P_GATES = 0
P_ATTN = 3072
P_Z = 4096
P_DQKV = 4608
P_RET = 6144
N_PROJ = 7680
ATTN_W = 1024
ATTN_QK = 768

TM = 512
FFN_CHUNK = 256
N_FFN_CHUNK = D_FF // FFN_CHUNK
PROJ_CHUNK = 768
MOD_TN = 1152
SEQ_TILE = 256
DN_CHUNK = 64
SEG_BLOCK = 256
VMEM_LIMIT = 56 * 1024 * 1024


def _dot(a, b, precision=None):
    return jnp.dot(a, b, preferred_element_type=F32, precision=precision)


def _dot_nt(a, b):
    return lax.dot_general(a, b, (((1,), (1,)), ((), ())), preferred_element_type=F32)


def _dot_tn(a, b):
    return lax.dot_general(a, b, (((0,), (0,)), ((), ())), preferred_element_type=F32)


def _split_bf16(a):
    hi = a.astype(BF16)
    return hi, (a - hi.astype(F32)).astype(BF16)


def _split3_bf16(a):
    h1 = a.astype(BF16)
    r1 = a - h1.astype(F32)
    h2 = r1.astype(BF16)
    h3 = (r1 - h2.astype(F32)).astype(BF16)
    return h1, h2, h3


def _dot_exact_rhs(m01, x):
    m = m01.astype(BF16)
    h1, h2, h3 = _split3_bf16(x)
    return _dot(m, h1) + (_dot(m, h2) + _dot(m, h3))


def _dot_exact_lhs(x, m01):
    m = m01.astype(BF16)
    h1, h2, h3 = _split3_bf16(x)
    return _dot(h1, m) + (_dot(h2, m) + _dot(h3, m))


def _sigmoid(x):
    return 1.0 / (1.0 + jnp.exp(-x))


def _silu(x):
    return x * _sigmoid(x)


def _modulate(x, gain, shift, scale):
    ms = jnp.mean(x * x, axis=-1, keepdims=True)
    return x * lax.rsqrt(ms + EPS) * (gain * (1.0 + scale)) + shift


def _seg_sum(x2, bd):
    hi = x2.astype(BF16)
    lo = (x2 - hi.astype(F32)).astype(BF16)
    blk = bd.shape[0]
    parts = [_dot(hi[:, i:i + blk], bd) + _dot(lo[:, i:i + blk], bd) for i in range(0, x2.shape[-1], blk)]
    return parts[0] if len(parts) == 1 else jnp.concatenate(parts, axis=1)


def _roll_lanes(x, shift):
    n = x.shape[-1]
    parts = [pltpu.roll(x[:, i:i + LANES], shift, 1) for i in range(0, n, LANES)]
    return parts[0] if len(parts) == 1 else jnp.concatenate(parts, axis=1)


def _rotate(x, c, a, b, sh):
    return x * c + _roll_lanes(x, LANES - sh) * a + _roll_lanes(x, sh) * b


def _cparams(*sem):
    return pltpu.CompilerParams(dimension_semantics=sem, vmem_limit_bytes=VMEM_LIMIT)


def _resident(shape):
    zeros = (0,) * len(shape)
    return pl.BlockSpec(shape, lambda *_: zeros, pipeline_mode=pl.Buffered(1))


def _mod_kernel(c_ref, w_ref, b_ref, o_ref):
    c = c_ref[...]
    o_ref[0] = _dot(_silu(c), w_ref[0], precision=HIGHEST) + b_ref[0]


def _mod_call(c, w_mod, b_mod):
    n = N_MOD * D_MODEL
    return pl.pallas_call(
        _mod_kernel,
        grid=(DEPTH, n // MOD_TN),
        in_specs=[
            pl.BlockSpec((BATCH, D_MODEL), lambda l, j: (0, 0)),
            pl.BlockSpec((1, D_MODEL, MOD_TN), lambda l, j: (l, 0, j)),
            pl.BlockSpec((1, 1, MOD_TN), lambda l, j: (l, 0, j)),
        ],
        out_specs=pl.BlockSpec((1, BATCH, MOD_TN), lambda l, j: (l, 0, j)),
        out_shape=jax.ShapeDtypeStruct((DEPTH, BATCH, n), F32),
        compiler_params=_cparams("parallel", "parallel"),
        name="adaln_mod",
    )(c, w_mod, b_mod.reshape(DEPTH, 1, n))


def _ffn_kernel(x_ref, mod_ref, gain_ref, w13_ref, w2_ref, o_ref, h_ref):
    x = x_ref[...]
    mod = mod_ref[0]
    ub = _modulate(x, gain_ref[...], mod[0:1], mod[1:2]).astype(BF16)
    for j in range(N_FFN_CHUNK):
        gu = _dot(ub, w13_ref[j])
        g = gu[:, :FFN_CHUNK]
        up = gu[:, FFN_CHUNK:]
        h_ref[:, j * FFN_CHUNK:(j + 1) * FFN_CHUNK] = (_silu(g) * up).astype(BF16)
    y = _dot(h_ref[...], w2_ref[...])
    o_ref[...] = x + (0.5 * mod[2:3]) * y


def _ffn_call(x2d, mod3, gain, w13r, w2):
    tiles_per_batch = SEQ // TM
    return pl.pallas_call(
        _ffn_kernel,
        grid=(N_TOK // TM,),
        in_specs=[
            pl.BlockSpec((TM, D_MODEL), lambda i: (i, 0)),
            pl.BlockSpec((1, 3, D_MODEL), lambda i: (i // tiles_per_batch, 0, 0)),
            _resident((1, D_MODEL)),
            _resident((N_FFN_CHUNK, D_MODEL, 2 * FFN_CHUNK)),
            _resident((D_FF, D_MODEL)),
        ],
        out_specs=pl.BlockSpec((TM, D_MODEL), lambda i: (i, 0)),
        out_shape=jax.ShapeDtypeStruct((N_TOK, D_MODEL), F32),
        scratch_shapes=[pltpu.VMEM((TM, D_FF), BF16)],
        compiler_params=_cparams("parallel"),
        name="swiglu_ffn",
    )(x2d, mod3, gain, w13r, w2)


def _inproj_kernel(x_ref, mod_ref, gain_ref, w_ref, wab_ref, wabt_ref, proj_ref, ab_ref, abt_ref):
    mod = mod_ref[0]
    ub = _modulate(x_ref[...], gain_ref[...], mod[0:1], mod[1:2]).astype(BF16)
    for j in range(N_PROJ // PROJ_CHUNK):
        sl = slice(j * PROJ_CHUNK, (j + 1) * PROJ_CHUNK)
        proj_ref[:, sl] = _dot(ub, w_ref[:, sl]).astype(BF16)
    ab_ref[...] = _dot(ub, wab_ref[...])
    abt_ref[0] = _dot_nt(wabt_ref[...], ub)


def _inproj_call(x2d, mod3, gain, w_in_p, w_ab, w_abt):
    tiles_per_batch = SEQ // TM
    return pl.pallas_call(
        _inproj_kernel,
        grid=(N_TOK // TM,),
        in_specs=[
            pl.BlockSpec((TM, D_MODEL), lambda i: (i, 0)),
            pl.BlockSpec((1, 3, D_MODEL), lambda i: (i // tiles_per_batch, 0, 0)),
            _resident((1, D_MODEL)),
            _resident((D_MODEL, N_PROJ)),
            _resident((D_MODEL, LANES)),
            _resident((2 * DN_HEADS, D_MODEL)),
        ],
        out_specs=[
            pl.BlockSpec((TM, N_PROJ), lambda i: (i, 0)),
            pl.BlockSpec((TM, LANES), lambda i: (i, 0)),
            pl.BlockSpec((1, 2 * DN_HEADS, TM), lambda i: (i // tiles_per_batch, 0, i % tiles_per_batch)),
        ],
        out_shape=[
            jax.ShapeDtypeStruct((N_TOK, N_PROJ), BF16),
            jax.ShapeDtypeStruct((N_TOK, LANES), F32),
            jax.ShapeDtypeStruct((BATCH, 2 * DN_HEADS, SEQ), F32),
        ],
        compiler_params=_cparams("parallel"),
        name="mixer_in_proj",
    )(x2d, mod3, gain, w_in_p, w_ab, w_abt)


def _head_norm(x, bd, gain):
    s = _seg_sum(x * x, bd)
    return x * lax.rsqrt(s * (1.0 / ATTN_HEAD_DIM) + EPS) * gain


def _attn_kernel(cur_ref, prev_ref, tc_ref, ta_ref, tb_ref, pc_ref, pa_ref, pb_ref,
                 gain_ref, bd_ref, sink_ref, o_ref):
    n = pl.program_id(1)
    blk = ATTN_BLOCK
    cur = cur_ref[0]
    prv = prev_ref[0]
    gain = gain_ref[...]
    qk = _rotate(_head_norm(cur[:, :ATTN_QK].astype(F32), bd_ref[...], gain),
                 tc_ref[...], ta_ref[...], tb_ref[...], ROPE_DIM // 2)
    kprev = _rotate(_head_norm(prv[:, 512:ATTN_QK].astype(F32), bd_ref[...], gain[:, 512:]),
                    pc_ref[...], pa_ref[...], pb_ref[...], ROPE_DIM // 2)
    kband = jnp.concatenate([kprev, qk[:, 512:]], axis=0).astype(BF16)
    vband = jnp.concatenate([prv[:, ATTN_QK:], cur[:, ATTN_QK:]], axis=0)

    qi = lax.broadcasted_iota(jnp.int32, (blk, 2 * blk), 0)
    kj = lax.broadcasted_iota(jnp.int32, (blk, 2 * blk), 1)
    valid = (kj > qi) & (kj <= qi + blk) & ((kj >= blk) | (n > 0))
    low_half = lax.broadcasted_iota(jnp.int32, (1, LANES), 1) < ATTN_HEAD_DIM
    sinks = sink_ref[...]

    for p in range(ATTN_Q_HEADS // 2):
        qp = qk[:, p * LANES:(p + 1) * LANES]
        grp = (2 * p) // 4
        kg = kband[:, grp * LANES:(grp + 1) * LANES]
        vg = vband[:, grp * LANES:(grp + 1) * LANES]
        outs = []
        for half in range(2):
            h = 2 * p + half
            mh = low_half if half == 0 else jnp.logical_not(low_half)
            qh = jnp.where(mh, qp, 0.0).astype(BF16)
            logits = _dot_nt(qh, kg) * (ATTN_HEAD_DIM ** -0.5)
            logits = jnp.where(valid, logits, NEG_INF)
            sink = sinks[h:h + 1, 0:1]
            m = jnp.maximum(jnp.max(logits, axis=1, keepdims=True), sink)
            e = jnp.exp(logits - m)
            den = jnp.sum(e, axis=1, keepdims=True) + jnp.exp(sink - m)
            probs = (e / den).astype(BF16)
            outs.append(_dot(probs, vg))
        o_ref[0, :, p * LANES:(p + 1) * LANES] = jnp.where(low_half, outs[0], outs[1]).astype(BF16)


def _attn_call(proj3, tabs, gain, bd, sinks):
    nb = SEQ // ATTN_BLOCK
    col = P_ATTN // ATTN_W
    prev = lambda n: jnp.maximum(n - 1, 0)
    tab_cur = pl.BlockSpec((ATTN_BLOCK, ATTN_QK), lambda b, n: (n, 0))
    tab_prev = pl.BlockSpec((ATTN_BLOCK, ATTN_QK - 512), lambda b, n: (prev(n), 2))
    return pl.pallas_call(
        _attn_kernel,
        grid=(BATCH, nb),
        in_specs=[
            pl.BlockSpec((1, ATTN_BLOCK, ATTN_W), lambda b, n: (b, n, col)),
            pl.BlockSpec((1, ATTN_BLOCK, ATTN_W), lambda b, n: (b, prev(n), col)),
            tab_cur, tab_cur, tab_cur, tab_prev, tab_prev, tab_prev,
            _resident((1, ATTN_QK)),
            _resident((SEG_BLOCK, SEG_BLOCK)),
            _resident((ATTN_Q_HEADS, LANES)),
        ],
        out_specs=pl.BlockSpec((1, ATTN_BLOCK, 512), lambda b, n: (b, n, 0)),
        out_shape=jax.ShapeDtypeStruct((BATCH, SEQ, 512), BF16),
        compiler_params=_cparams("parallel", "parallel"),
        name="swa_attention",
    )(proj3, proj3, tabs[0], tabs[1], tabs[2], tabs[0], tabs[1], tabs[2], gain, bd, sinks)


def _softplus(x):
    return jnp.maximum(x, 0.0) + jnp.log(1.0 + jnp.exp(-jnp.abs(x)))


def _dn_kernel(cur_ref, prev_ref, z_ref, ab_ref, abt_ref, conv_ref, alog_ref, dtb_ref,
               alogt_ref, dtbt_ref, onorm_ref, bd_ref, o_ref, state_ref):
    t = pl.program_id(1)
    T = SEQ_TILE
    C = DN_CHUNK
    nh = DN_HEADS
    width = nh * DN_DIM

    @pl.when(t == 0)
    def _():
        state_ref[...] = jnp.zeros_like(state_ref)

    x = cur_ref[0].astype(F32)
    tail = jnp.where(t > 0, prev_ref[0].astype(F32), 0.0)
    xc = jnp.concatenate([tail, x], axis=0)
    w = conv_ref[...]
    y = xc[SUBLANES:SUBLANES + T] * w[3:4]
    for j in range(DN_CONV - 1):
        off = SUBLANES - (DN_CONV - 1) + j
        y = y + xc[off:off + T] * w[j:j + 1]
    y = _silu(y)
    bd = bd_ref[...]
    q = y[:, :width]
    k = y[:, width:2 * width]
    v = y[:, 2 * width:]
    q = q * lax.rsqrt(_seg_sum(q * q, bd) + EPS) * (DN_DIM ** -0.5)
    k = k * lax.rsqrt(_seg_sum(k * k, bd) + EPS)

    ab = ab_ref[...]
    la = -jnp.exp(alog_ref[...]) * _softplus(ab + dtb_ref[...])
    beta = _sigmoid(ab)
    lat = -jnp.exp(alogt_ref[...]) * _softplus(abt_ref[0] + dtbt_ref[...])
    ri = lax.broadcasted_iota(jnp.int32, (T, T), 0)
    ci = lax.broadcasted_iota(jnp.int32, (T, T), 1)
    same = (ri // C) == (ci // C)
    causal = same & (ci <= ri)
    strict = same & (ci < ri)
    eye = (ri == ci).astype(F32)
    g_nat = _dot_exact_rhs(causal, la)
    g_t = _dot_exact_lhs(lat, same & (ri <= ci))
    gend_nat = jnp.concatenate(
        [jnp.broadcast_to(g_nat[c * C + C - 1:c * C + C, :], (C, LANES)) for c in range(T // C)], axis=0)

    low_half = lax.broadcasted_iota(jnp.int32, (1, LANES), 1) < DN_DIM
    high_half = jnp.logical_not(low_half)
    bdr = lax.broadcasted_iota(jnp.int32, (LANES, LANES), 0) // DN_DIM
    bdc = lax.broadcasted_iota(jnp.int32, (LANES, LANES), 1) // DN_DIM
    blockdiag = bdr == bdc

    def per_head_lanes(src, first):
        return jnp.where(low_half, src[:, first:first + 1], src[:, first + 1:first + 2])

    npair = nh // 2
    halves = (low_half, high_half)
    qd, kt, dec_end, rhs, pw, xinv, intras = [], [], [], [], [], [], []
    for p in range(npair):
        sl = slice(p * LANES, (p + 1) * LANES)
        gx = per_head_lanes(g_nat, 2 * p)
        gendx = per_head_lanes(gend_nat, 2 * p)
        bx = per_head_lanes(beta, nh + 2 * p)
        eg = jnp.exp(gx)
        k_p = k[:, sl]
        q_p = q[:, sl]
        kb = k_p * bx
        vb = v[:, sl] * bx
        kbe = kb * eg
        qd.append((q_p * eg).astype(BF16))
        kt.append((k_p * jnp.exp(gendx - gx)).astype(BF16))
        dec_end.append(jnp.exp(gendx))
        k_pb = k_p.astype(BF16)
        for half in range(2):
            h = 2 * p + half
            mh = halves[half]
            diff = g_nat[:, h:h + 1] - g_t[h:h + 1, :]
            decay = jnp.exp(jnp.where(causal, diff, NEG_INF))
            kk = _dot_nt(jnp.where(mh, kb, 0.0).astype(BF16), k_pb)
            qkm = _dot_nt(jnp.where(mh, q_p, 0.0).astype(BF16), k_pb)
            intras.append((qkm * decay).astype(BF16))
            n1 = jnp.where(strict, -(kk * decay), 0.0).astype(BF16)
            pw.append(n1)
            xinv.append(eye + n1.astype(F32))
            rhs.append(jnp.concatenate([jnp.where(mh, vb, 0.0), jnp.where(mh, kbe, 0.0)], axis=1))
    for _ in range(5):
        for h in range(nh):
            pw[h] = _dot(pw[h], pw[h]).astype(BF16)
            xinv[h] = xinv[h] + _dot(xinv[h].astype(BF16), pw[h])
    ys = []
    for h in range(nh):
        xb = xinv[h].astype(BF16)
        rh, rl = _split_bf16(rhs[h])
        ys.append(_dot(xb, rh) + _dot(xb, rl))
    u, w_ = [], []
    for p in range(npair):
        u.append(ys[2 * p][:, :LANES] + ys[2 * p + 1][:, :LANES])
        w_.append((ys[2 * p][:, LANES:] + ys[2 * p + 1][:, LANES:]).astype(BF16))
    state = [state_ref[p] for p in range(npair)]
    vn = [[] for _ in range(npair)]
    oi = [[] for _ in range(npair)]
    for c in range(T // C):
        rows = slice(c * C, (c + 1) * C)
        for p in range(npair):
            sb = state[p].astype(BF16)
            vn_c = u[p][rows] - _dot(w_[p][rows], sb)
            oi[p].append(_dot(qd[p][rows], sb))
            upd = _dot_tn(kt[p][rows], vn_c.astype(BF16))
            state[p] = state[p] * dec_end[p][c * C:c * C + 1] + jnp.where(blockdiag, upd, 0.0)
            vn[p].append(vn_c)
    outs = []
    for p in range(npair):
        state_ref[p] = state[p]
        vn_all = jnp.concatenate(vn[p], axis=0)
        o_p = jnp.concatenate(oi[p], axis=0)
        for half in range(2):
            o_p = o_p + _dot(intras[2 * p + half], jnp.where(halves[half], vn_all, 0.0).astype(BF16))
        outs.append(o_p)
    o = jnp.concatenate(outs, axis=1)
    ms = _seg_sum(o * o, bd) * (1.0 / DN_DIM)
    zg = _silu(z_ref[0].astype(F32))
    o_ref[0] = (o * lax.rsqrt(ms + EPS) * onorm_ref[...] * zg).astype(BF16)


def _dn_call(proj3, ab, abt, conv_w, alog, dtb, alog_t, dtb_t, onorm, bd):
    T = SEQ_TILE
    rows8 = T // SUBLANES
    return pl.pallas_call(
        _dn_kernel,
        grid=(BATCH, SEQ // T),
        in_specs=[
            pl.BlockSpec((1, T, 1536), lambda b, t: (b, t, P_DQKV // 1536)),
            pl.BlockSpec((1, SUBLANES, 1536), lambda b, t: (b, jnp.maximum(t * rows8 - 1, 0), P_DQKV // 1536)),
            pl.BlockSpec((1, T, 512), lambda b, t: (b, t, P_Z // 512)),
            pl.BlockSpec((T, LANES), lambda b, t: (b * (SEQ // T) + t, 0)),
            pl.BlockSpec((1, 2 * DN_HEADS, T), lambda b, t: (b, 0, t)),
            _resident((DN_CONV, 1536)),
            _resident((1, LANES)),
            _resident((1, LANES)),
            _resident((2 * DN_HEADS, T)),
            _resident((2 * DN_HEADS, T)),
            _resident((1, 512)),
            _resident((SEG_BLOCK, SEG_BLOCK)),
        ],
        out_specs=pl.BlockSpec((1, T, 512), lambda b, t: (b, t, 0)),
        out_shape=jax.ShapeDtypeStruct((BATCH, SEQ, 512), BF16),
        scratch_shapes=[pltpu.VMEM((DN_HEADS // 2, LANES, LANES), F32)],
        compiler_params=_cparams("parallel", "arbitrary"),
        name="gated_deltanet",
    )(proj3, proj3, proj3, ab, abt, conv_w, alog, dtb, alog_t, dtb_t, onorm, bd)


def _ret_kernel(x_ref, tc_ref, ta_ref, tb_ref, o_ref, state_ref):
    t = pl.program_id(1)
    T = SEQ_TILE

    @pl.when(t == 0)
    def _():
        state_ref[...] = jnp.zeros_like(state_ref)

    x = x_ref[0]
    qk = _rotate(x[:, :512].astype(F32), tc_ref[...], ta_ref[...], tb_ref[...], 1)
    rq = qk[:, :256]
    rk = qk[:, 256:] * (RET_KEY_DIM ** -0.5)
    ri = lax.broadcasted_iota(jnp.int32, (T, T), 0)
    ci = lax.broadcasted_iota(jnp.int32, (T, T), 1)
    causal = ci <= ri
    dist = (ri - ci).astype(F32)
    row = lax.broadcasted_iota(jnp.int32, (T, LANES), 0).astype(F32)
    low_half = lax.broadcasted_iota(jnp.int32, (1, LANES), 1) < RET_KEY_DIM

    for h in range(RET_HEADS):
        log_gamma = math.log1p(-(2.0 ** (-5.0 - h)))
        p = h // 2
        mh = low_half if h % 2 == 0 else jnp.logical_not(low_half)
        q_p = rq[:, p * LANES:(p + 1) * LANES]
        k_p = rk[:, p * LANES:(p + 1) * LANES]
        v_h = x[:, 512 + h * LANES:512 + (h + 1) * LANES]
        decay = jnp.exp(jnp.where(causal, dist * log_gamma, NEG_INF))
        scores = _dot_nt(jnp.where(mh, q_p, 0.0).astype(BF16), k_p.astype(BF16)) * decay
        state = state_ref[h]
        q_dec = jnp.where(mh, q_p * jnp.exp(log_gamma * (row + 1.0)), 0.0)
        out = _dot(scores.astype(BF16), v_h) + _dot(q_dec.astype(BF16), state.astype(BF16))
        k_end = jnp.where(mh, k_p * jnp.exp(log_gamma * (T - 1.0 - row)), 0.0)
        state_ref[h] = state * math.exp(log_gamma * T) + _dot_tn(k_end.astype(BF16), v_h)
        ms = jnp.mean(out * out, axis=-1, keepdims=True)
        g = _silu(x[:, 1024 + h * LANES:1024 + (h + 1) * LANES].astype(F32))
        o_ref[0, :, h * LANES:(h + 1) * LANES] = (out * lax.rsqrt(ms + EPS) * g).astype(BF16)


def _ret_call(proj3, tabs):
    T = SEQ_TILE
    tab = pl.BlockSpec((T, 512), lambda b, t: (t, 0))
    return pl.pallas_call(
        _ret_kernel,
        grid=(BATCH, SEQ // T),
        in_specs=[pl.BlockSpec((1, T, 1536), lambda b, t: (b, t, P_RET // 1536)), tab, tab, tab],
        out_specs=pl.BlockSpec((1, T, 512), lambda b, t: (b, t, 0)),
        out_shape=jax.ShapeDtypeStruct((BATCH, SEQ, 512), BF16),
        scratch_shapes=[pltpu.VMEM((RET_HEADS, LANES, LANES), F32)],
        compiler_params=_cparams("parallel", "arbitrary"),
        name="retention",
    )(proj3, tabs[0], tabs[1], tabs[2])


def _merge_kernel(x_ref, mod_ref, a_ref, b_ref, c_ref, g_ref, wb_ref, wo_ref, o_ref):
    merged = jnp.zeros((TM, D_MODEL), F32)
    for i, br in enumerate((a_ref, b_ref, c_ref)):
        gate = _sigmoid(g_ref[:, i * D_MODEL:(i + 1) * D_MODEL].astype(F32))
        merged = merged + gate * _dot(br[...], wb_ref[i])
    y = _dot(merged.astype(BF16), wo_ref[...])
    o_ref[...] = x_ref[...] + mod_ref[0][2:3] * y


def _merge_call(x2d, mod3, out_a, out_b, out_c, proj2, w_branch, w_out):
    tiles_per_batch = SEQ // TM
    branch = pl.BlockSpec((TM, 512), lambda i: (i, 0))
    return pl.pallas_call(
        _merge_kernel,
        grid=(N_TOK // TM,),
        in_specs=[
            pl.BlockSpec((TM, D_MODEL), lambda i: (i, 0)),
            pl.BlockSpec((1, 3, D_MODEL), lambda i: (i // tiles_per_batch, 0, 0)),
            branch, branch, branch,
            pl.BlockSpec((TM, 3 * D_MODEL), lambda i: (i, P_GATES // (3 * D_MODEL))),
            _resident((3, 512, D_MODEL)),
            _resident((D_MODEL, D_MODEL)),
        ],
        out_specs=pl.BlockSpec((TM, D_MODEL), lambda i: (i, 0)),
        out_shape=jax.ShapeDtypeStruct((N_TOK, D_MODEL), F32),
        compiler_params=_cparams("parallel"),
        name="branch_merge",
    )(x2d, mod3, out_a, out_b, out_c, proj2, w_branch, w_out)


def _block_diag_ones(n, seg):
    idx = np.arange(n) // seg
    return jnp.asarray((idx[:, None] == idx[None, :]).astype(np.float32), dtype=BF16)


def _attn_tables():
    half = ROPE_DIM // 2
    pos = jnp.arange(SEQ, dtype=F32)
    inv_freq = ROPE_THETA ** (-jnp.arange(0, ROPE_DIM, 2, dtype=F32) / ROPE_DIM)
    phase = pos[:, None] * inv_freq[None, :]
    cos, sin = jnp.cos(phase), jnp.sin(phase)
    pad = ATTN_HEAD_DIM - ROPE_DIM
    c = jnp.concatenate([cos, cos, jnp.ones((SEQ, pad), F32)], axis=1)
    a = jnp.concatenate([-sin, jnp.zeros((SEQ, ATTN_HEAD_DIM - half), F32)], axis=1)
    b = jnp.concatenate([jnp.zeros((SEQ, half), F32), sin, jnp.zeros((SEQ, pad), F32)], axis=1)
    reps = ATTN_QK // ATTN_HEAD_DIM
    return tuple(jnp.tile(t, (1, reps)) for t in (c, a, b))


def _ret_tables():
    pos = jnp.arange(SEQ, dtype=F32)
    angle = 1.0 / (RET_THETA ** jnp.linspace(0.0, 1.0, RET_KEY_DIM // 2, dtype=F32))
    angle = jnp.repeat(angle, 2)
    phase = pos[:, None] * angle[None, :]
    cos, sin = jnp.cos(phase), jnp.sin(phase)
    even = (jnp.arange(RET_KEY_DIM) % 2 == 0)[None, :]
    a = jnp.where(even, -sin, 0.0)
    b = jnp.where(even, 0.0, sin)
    reps = 512 // RET_KEY_DIM
    return tuple(jnp.tile(t, (1, reps)) for t in (cos, a, b))


def _relayout_w_in(w):
    cs = lambda a, b: w[:, a:b]
    k0, k1 = cs(512, 576), cs(576, 640)
    v0, v1 = cs(640, 704), cs(704, 768)
    cols = [cs(4368, 7440), cs(0, 512), k0, k0, k1, k1, v0, v0, v1, v1,
            cs(2320, 2832), cs(768, 2304), cs(2832, 3344), cs(3344, 3856), cs(3856, 4368)]
    w_p = jnp.concatenate(cols, axis=1).astype(BF16)
    ab = cs(2304, 2320)
    w_ab = jnp.concatenate([ab, jnp.zeros((D_MODEL, LANES - 2 * DN_HEADS), F32)], axis=1).astype(BF16)
    return w_p, w_ab, ab.T.astype(BF16)


def _relayout_w13(w13):
    g = w13[:, :D_FF].reshape(D_MODEL, N_FFN_CHUNK, FFN_CHUNK)
    u = w13[:, D_FF:].reshape(D_MODEL, N_FFN_CHUNK, FFN_CHUNK)
    return jnp.concatenate([g, u], axis=2).transpose(1, 0, 2).astype(BF16)


def _lane_row(vec, first_lane=0):
    row = jnp.zeros((1, LANES), F32)
    return row.at[0, first_lane:first_lane + vec.shape[0]].set(vec)


def _rows_t(vec):
    col = jnp.concatenate([vec, jnp.zeros((2 * DN_HEADS - vec.shape[0],), F32)])
    return jnp.broadcast_to(col[:, None], (2 * DN_HEADS, SEQ_TILE))


def kernel(x, c, w_mod, b_mod, ffn1_norm, ffn1_w13, ffn1_w2, mix_norm, w_in, attn_q_norm, attn_k_norm,
           attn_sinks, dn_conv, dn_a_log, dn_dt_bias, dn_out_norm, w_branch, w_out, ffn2_norm, ffn2_w13,
           ffn2_w2):
    mod = _mod_call(c, w_mod, b_mod).reshape(DEPTH, BATCH, N_MOD, D_MODEL)
    attn_tabs = _attn_tables()
    ret_tabs = _ret_tables()
    bd = _block_diag_ones(SEG_BLOCK, ATTN_HEAD_DIM)

    xf = x.reshape(N_TOK, D_MODEL)
    for l in range(DEPTH):
        xf = _ffn_call(xf, mod[l, :, 0:3], ffn1_norm[l][None], _relayout_w13(ffn1_w13[l]),
                       ffn1_w2[l].astype(BF16))

        w_in_p, w_ab, w_abt = _relayout_w_in(w_in[l])
        proj, ab, abt = _inproj_call(xf, mod[l, :, 3:6], mix_norm[l][None], w_in_p, w_ab, w_abt)
        proj3 = proj.reshape(BATCH, SEQ, N_PROJ)
        qk_gain = jnp.concatenate([jnp.tile(attn_q_norm[l], ATTN_Q_HEADS),
                                   jnp.tile(attn_k_norm[l], (ATTN_QK - 512) // ATTN_HEAD_DIM)])[None]
        sinks = jnp.broadcast_to(attn_sinks[l][:, None], (ATTN_Q_HEADS, LANES))
        out_a = _attn_call(proj3, attn_tabs, qk_gain, bd, sinks)
        out_b = _dn_call(proj3, ab, abt, dn_conv[l], _lane_row(dn_a_log[l]), _lane_row(dn_dt_bias[l]),
                         _rows_t(dn_a_log[l]), _rows_t(dn_dt_bias[l]),
                         jnp.tile(dn_out_norm[l], DN_HEADS)[None], bd)
        out_c = _ret_call(proj3, ret_tabs)
        xf = _merge_call(xf, mod[l, :, 3:6], out_a.reshape(N_TOK, 512), out_b.reshape(N_TOK, 512),
                         out_c.reshape(N_TOK, 512), proj, w_branch[l].astype(BF16), w_out[l].astype(BF16))

        xf = _ffn_call(xf, mod[l, :, 6:9], ffn2_norm[l][None], _relayout_w13(ffn2_w13[l]),
                       ffn2_w2[l].astype(BF16))
    return xf.reshape(BATCH, SEQ, D_MODEL)
```

```python
import math

import numpy as np
import jax
import jax.numpy as jnp
from jax import lax
from jax.experimental import pallas as pl
from jax.experimental.pallas import tpu as pltpu

F32 = jnp.float32
BF16 = jnp.bfloat16
HIGHEST = lax.Precision.HIGHEST

D_MODEL = 1024
BATCH = 8
SEQ = 2048
DEPTH = 2
N_TOK = BATCH * SEQ
N_MOD = 9
D_FF = 2816
EPS = 1e-6
NEG_INF = -1e30

ATTN_Q_HEADS = 8
ATTN_KV_HEADS = 2
ATTN_HEAD_DIM = 64
ATTN_BLOCK = 128
ROPE_DIM = 16
ROPE_THETA = 500000.0
DN_HEADS = 8
DN_DIM = 64
DN_CONV = 4
RET_HEADS = 4
RET_KEY_DIM = 64
RET_THETA = 10000.0

LANES = 128
SUBLANES = 8
MXU_TILE = 256

P_GATES = 0
P_RET = 3072
P_DQKV = 4608
P_Z = 6144
P_ATTN = 6912
N_PROJ = 7680
ATTN_W = 768
ATTN_QK = 640

TM = 512
FFN_CHUNK = 256
N_FFN_CHUNK = D_FF // FFN_CHUNK
PROJ_CHUNK = 768
MOD_TN = 1152
SEQ_TILE = 256
ATTN_TILE = 512
DN_CHUNK = 64
VMEM_LIMIT = 56 * 1024 * 1024


def _dot(a, b, precision=None):
    return jnp.dot(a, b, preferred_element_type=F32, precision=precision)


def _dot_nt(a, b):
    return lax.dot_general(a, b, (((1,), (1,)), ((), ())), preferred_element_type=F32)


def _dot_tn(a, b):
    return lax.dot_general(a, b, (((0,), (0,)), ((), ())), preferred_element_type=F32)


def _split_bf16(a):
    hi = a.astype(BF16)
    return hi, (a - hi.astype(F32)).astype(BF16)


def _split3_bf16(a):
    h1 = a.astype(BF16)
    r1 = a - h1.astype(F32)
    h2 = r1.astype(BF16)
    h3 = (r1 - h2.astype(F32)).astype(BF16)
    return h1, h2, h3


def _dot_exact_rhs(m01, x):
    m = m01.astype(BF16)
    h1, h2, h3 = _split3_bf16(x)
    return _dot(m, h1) + (_dot(m, h2) + _dot(m, h3))


def _dot_exact_lhs(x, m01):
    m = m01.astype(BF16)
    h1, h2, h3 = _split3_bf16(x)
    return _dot(h1, m) + (_dot(h2, m) + _dot(h3, m))


def _sigmoid(x):
    return 1.0 / (1.0 + jnp.exp(-x))


def _silu(x):
    return x * _sigmoid(x)


def _modulate(x, gain, shift, scale):
    ms = jnp.mean(x * x, axis=-1, keepdims=True)
    return x * lax.rsqrt(ms + EPS) * (gain * (1.0 + scale)) + shift


def _lane_block_matmul(x, m):
    hi, lo = _split_bf16(x)
    blk = m.shape[0]
    outs = []
    for s in range(0, x.shape[-1], blk):
        n = min(blk, x.shape[-1] - s)
        mm = m if n == blk else m[:n, :n]
        outs.append(_dot(hi[:, s:s + n], mm) + _dot(lo[:, s:s + n], mm))
    return outs[0] if len(outs) == 1 else jnp.concatenate(outs, axis=1)


def _seg_sum(x2, bd):
    return _lane_block_matmul(x2, bd)


def _tile_lanes(t, width):
    reps = width // t.shape[-1]
    return t if reps == 1 else jnp.concatenate([t] * reps, axis=1)


def _rotary(x, perm, cos, sin):
    w = x.shape[-1]
    return x * _tile_lanes(cos, w) + _lane_block_matmul(x, perm) * _tile_lanes(sin, w)


def _cparams(*sem):
    return pltpu.CompilerParams(dimension_semantics=sem, vmem_limit_bytes=VMEM_LIMIT)


def _resident(shape):
    zeros = (0,) * len(shape)
    return pl.BlockSpec(shape, lambda *_: zeros, pipeline_mode=pl.Buffered(1))


def _layer_resident(shape, layer):
    idx = (layer,) + (0,) * len(shape)
    return pl.BlockSpec((1,) + tuple(shape), lambda *_: idx, pipeline_mode=pl.Buffered(1))


def _mod_spec(layer, group, batch_of):
    return pl.BlockSpec((1, 1, 1, 3, D_MODEL), lambda *g: (layer, batch_of(*g), group, 0, 0))


def _mod_kernel(c_ref, w_ref, b_ref, o_ref):
    c = c_ref[...]
    o_ref[0] = _dot(_silu(c), w_ref[0], precision=HIGHEST) + b_ref[0]


def _mod_call(c, w_mod, b_mod):
    n = N_MOD * D_MODEL
    return pl.pallas_call(
        _mod_kernel,
        grid=(DEPTH, n // MOD_TN),
        in_specs=[
            pl.BlockSpec((BATCH, D_MODEL), lambda l, j: (0, 0)),
            pl.BlockSpec((1, D_MODEL, MOD_TN), lambda l, j: (l, 0, j)),
            pl.BlockSpec((1, 1, MOD_TN), lambda l, j: (l, 0, j)),
        ],
        out_specs=pl.BlockSpec((1, BATCH, MOD_TN), lambda l, j: (l, 0, j)),
        out_shape=jax.ShapeDtypeStruct((DEPTH, BATCH, n), F32),
        compiler_params=_cparams("parallel", "parallel"),
        name="adaln_mod",
    )(c, w_mod, b_mod.reshape(DEPTH, 1, n))


def _ffn_kernel(x_ref, mod_ref, gain_ref, w13_ref, w2_ref, o_ref, h_ref):
    x = x_ref[...]
    mod = mod_ref[0, 0, 0]
    ub = _modulate(x, gain_ref[0], mod[0:1], mod[1:2]).astype(BF16)
    for j in range(N_FFN_CHUNK):
        lo = j * FFN_CHUNK
        g = _dot(ub, w13_ref[0, :, lo:lo + FFN_CHUNK])
        up = _dot(ub, w13_ref[0, :, D_FF + lo:D_FF + lo + FFN_CHUNK])
        h_ref[:, lo:lo + FFN_CHUNK] = (_silu(g) * up).astype(BF16)
    y = _dot(h_ref[...], w2_ref[0])
    o_ref[...] = x + (0.5 * mod[2:3]) * y


def _ffn_call(x2d, mod5, layer, group, gains, w13, w2):
    tiles_per_batch = SEQ // TM
    return pl.pallas_call(
        _ffn_kernel,
        grid=(N_TOK // TM,),
        in_specs=[
            pl.BlockSpec((TM, D_MODEL), lambda i: (i, 0)),
            _mod_spec(layer, group, lambda i: i // tiles_per_batch),
            _layer_resident((1, D_MODEL), layer),
            _layer_resident((D_MODEL, 2 * D_FF), layer),
            _layer_resident((D_FF, D_MODEL), layer),
        ],
        out_specs=pl.BlockSpec((TM, D_MODEL), lambda i: (i, 0)),
        out_shape=jax.ShapeDtypeStruct((N_TOK, D_MODEL), F32),
        scratch_shapes=[pltpu.VMEM((TM, D_FF), BF16)],
        compiler_params=_cparams("parallel"),
        name="swiglu_ffn",
    )(x2d, mod5, gains, w13, w2)


def _inproj_kernel(x_ref, mod_ref, gain_ref, w_ref, wab_ref, wabt_ref, proj_ref, ab_ref, abt_ref):
    mod = mod_ref[0, 0, 0]
    ub = _modulate(x_ref[...], gain_ref[0], mod[0:1], mod[1:2]).astype(BF16)
    for j in range(N_PROJ // PROJ_CHUNK):
        sl = slice(j * PROJ_CHUNK, (j + 1) * PROJ_CHUNK)
        proj_ref[:, sl] = _dot(ub, w_ref[0, :, sl]).astype(BF16)
    ab_ref[...] = _dot(ub, wab_ref[0])
    abt_ref[0] = _dot_nt(wabt_ref[0], ub)


def _inproj_call(x2d, mod5, layer, gains, w_in_p, w_ab, w_abt):
    tiles_per_batch = SEQ // TM
    return pl.pallas_call(
        _inproj_kernel,
        grid=(N_TOK // TM,),
        in_specs=[
            pl.BlockSpec((TM, D_MODEL), lambda i: (i, 0)),
            _mod_spec(layer, 1, lambda i: i // tiles_per_batch),
            _layer_resident((1, D_MODEL), layer),
            _layer_resident((D_MODEL, N_PROJ), layer),
            _layer_resident((D_MODEL, LANES), layer),
            _layer_resident((2 * DN_HEADS, D_MODEL), layer),
        ],
        out_specs=[
            pl.BlockSpec((TM, N_PROJ), lambda i: (i, 0)),
            pl.BlockSpec((TM, LANES), lambda i: (i, 0)),
            pl.BlockSpec((1, 2 * DN_HEADS, TM), lambda i: (i // tiles_per_batch, 0, i % tiles_per_batch)),
        ],
        out_shape=[
            jax.ShapeDtypeStruct((N_TOK, N_PROJ), BF16),
            jax.ShapeDtypeStruct((N_TOK, LANES), F32),
            jax.ShapeDtypeStruct((BATCH, 2 * DN_HEADS, SEQ), F32),
        ],
        compiler_params=_cparams("parallel"),
        name="mixer_in_proj",
    )(x2d, mod5, gains, w_in_p, w_ab, w_abt)


def _attn_kernel(cur_ref, prev_ref, tc_ref, ts_ref, pc_ref, ps_ref, gain_ref, bd_ref, perm_ref,
                 sink_ref, o_ref):
    t = pl.program_id(1)
    blk = ATTN_BLOCK
    nblk = ATTN_TILE // blk
    per_group = ATTN_Q_HEADS // ATTN_KV_HEADS
    bd = bd_ref[...]
    perm = perm_ref[...]
    gain = gain_ref[0]

    def norm_rope(x, g, cos, sin):
        x = x * lax.rsqrt(_seg_sum(x * x, bd) * (1.0 / ATTN_HEAD_DIM) + EPS) * g
        return _rotary(x, perm, cos, sin)

    cur = cur_ref[0]
    prv = prev_ref[0]
    qk = norm_rope(cur[:, :ATTN_QK].astype(F32), gain, tc_ref[...], ts_ref[...])
    kprev = norm_rope(prv[:, 512:ATTN_QK].astype(F32), gain[:, 512:], pc_ref[...], ps_ref[...])
    q = qk[:, :512] * (ATTN_HEAD_DIM ** -0.5)
    k_all = jnp.concatenate([kprev, qk[:, 512:]], axis=0)
    v_all = jnp.concatenate([prv[:, ATTN_QK:], cur[:, ATTN_QK:]], axis=0).astype(F32)

    low_half = lax.broadcasted_iota(jnp.int32, (1, LANES), 1) < ATTN_HEAD_DIM

    def group_on_both_halves(a, g):
        swapped = pltpu.roll(a, ATTN_HEAD_DIM, 1)
        return (jnp.where(low_half, a, swapped) if g == 0 else jnp.where(low_half, swapped, a)).astype(BF16)

    kd = [group_on_both_halves(k_all, g) for g in range(ATTN_KV_HEADS)]
    vd = [group_on_both_halves(v_all, g) for g in range(ATTN_KV_HEADS)]

    kj = lax.broadcasted_iota(jnp.int32, (2 * blk, blk), 0)
    qi = lax.broadcasted_iota(jnp.int32, (2 * blk, blk), 1)
    in_window = (kj > qi) & (kj <= qi + blk)
    bias = jnp.where(in_window, 0.0, NEG_INF)
    bias_no_prev = jnp.where(in_window & (kj >= blk), 0.0, NEG_INF)
    bias_rest = jnp.concatenate([bias] * per_group, axis=1)
    bias_first = jnp.where(t == 0, jnp.concatenate([bias_no_prev] * per_group, axis=1), bias_rest)
    sink_row = lax.broadcasted_iota(jnp.int32, (2 * blk, per_group * blk), 0) == 0
    sinks = sink_ref[0]
    ones_blk = jnp.ones((2 * blk, LANES), BF16)
    first_key = lax.broadcasted_iota(jnp.int32, (2 * blk, LANES), 0) == 0

    units = [(i, g) for i in range(nblk) for g in range(ATTN_KV_HEADS)]
    logits = []
    for i, g in units:
        rows = slice(i * blk, (i + 1) * blk)
        parts = []
        for p in range(g * per_group // 2, (g + 1) * per_group // 2):
            qp = q[rows, p * LANES:(p + 1) * LANES]
            parts += [jnp.where(low_half, qp, 0.0), jnp.where(low_half, 0.0, qp)]
        qs = jnp.concatenate(parts, axis=0).astype(BF16)
        band = kd[g][i * blk:(i + 2) * blk]
        sink = jnp.concatenate(
            [jnp.broadcast_to(sinks[g * per_group + r:g * per_group + r + 1, :], (2 * blk, LANES))
             for r in range(per_group)], axis=1)
        lg = _dot_nt(band, qs) + (bias_first if i == 0 else bias_rest)
        logits.append(jnp.where(sink_row, sink, lg))
    expd = []
    for lg in logits:
        m = jnp.max(lg, axis=0, keepdims=True)
        expd.append(jnp.exp(lg - m).astype(BF16))
    for (i, g), e in zip(units, expd):
        vals = jnp.where(first_key, 0.0, vd[g][i * blk:(i + 2) * blk].astype(F32)).astype(BF16)
        o = _dot_tn(e, jnp.concatenate([vals, ones_blk], axis=1))
        o = o[:, :LANES] * (1.0 / o[:, LANES:])
        for j in range(per_group // 2):
            p = g * per_group // 2 + j
            pair = jnp.where(low_half, o[2 * j * blk:(2 * j + 1) * blk], o[(2 * j + 1) * blk:(2 * j + 2) * blk])
            o_ref[0, i * blk:(i + 1) * blk, p * LANES:(p + 1) * LANES] = pair.astype(BF16)


def _attn_call(proj3, layer, tabs, gains, bd, perm, sinks):
    nblk = ATTN_TILE // ATTN_BLOCK
    col = P_ATTN // ATTN_W
    tab_cur = pl.BlockSpec((ATTN_TILE, LANES), lambda b, t: (t, 0))
    tab_prev = pl.BlockSpec((ATTN_BLOCK, LANES), lambda b, t: (jnp.maximum(t * nblk - 1, 0), 0))
    return pl.pallas_call(
        _attn_kernel,
        grid=(BATCH, SEQ // ATTN_TILE),
        in_specs=[
            pl.BlockSpec((1, ATTN_TILE, ATTN_W), lambda b, t: (b, t, col)),
            pl.BlockSpec((1, ATTN_BLOCK, ATTN_W), lambda b, t: (b, jnp.maximum(t * nblk - 1, 0), col)),
            tab_cur, tab_cur, tab_prev, tab_prev,
            _layer_resident((1, ATTN_QK), layer),
            _resident((MXU_TILE, MXU_TILE)),
            _resident((MXU_TILE, MXU_TILE)),
            _layer_resident((ATTN_Q_HEADS, LANES), layer),
        ],
        out_specs=pl.BlockSpec((1, ATTN_TILE, 512), lambda b, t: (b, t, 0)),
        out_shape=jax.ShapeDtypeStruct((BATCH, SEQ, 512), BF16),
        compiler_params=_cparams("parallel", "parallel"),
        name="swa_attention",
    )(proj3, proj3, tabs[0], tabs[1], tabs[0], tabs[1], gains, bd, perm, sinks)


def _softplus(x):
    return jnp.maximum(x, 0.0) + jnp.log(1.0 + jnp.exp(-jnp.abs(x)))


def _dn_kernel(cur_ref, prev_ref, z_ref, ab_ref, abt_ref, conv_ref, alog_ref, dtb_ref,
               alogt_ref, dtbt_ref, onorm_ref, bd_ref, o_ref, state_ref, y_ref, c_ref, r_ref):
    t = pl.program_id(1)
    T = SEQ_TILE
    C = DN_CHUNK
    nh = DN_HEADS
    width = nh * DN_DIM

    @pl.when(t == 0)
    def _():
        state_ref[...] = jnp.zeros_like(state_ref)

    x = cur_ref[0].astype(F32)
    tail = jnp.where(t > 0, prev_ref[0].astype(F32)[SUBLANES:], 0.0)
    xc = jnp.concatenate([tail, x], axis=0)
    w = conv_ref[0]
    y = xc[SUBLANES:SUBLANES + T] * w[3:4]
    for j in range(DN_CONV - 1):
        off = SUBLANES - (DN_CONV - 1) + j
        y = y + xc[off:off + T] * w[j:j + 1]
    y = _silu(y)
    bd = bd_ref[...]
    q = y[:, :width]
    k = y[:, width:2 * width]
    v = y[:, 2 * width:]
    q = q * lax.rsqrt(_seg_sum(q * q, bd) + EPS) * (DN_DIM ** -0.5)
    k = k * lax.rsqrt(_seg_sum(k * k, bd) + EPS)

    ab = ab_ref[...]
    la = -jnp.exp(alog_ref[0]) * _softplus(ab + dtb_ref[0])
    beta = _sigmoid(ab)
    lat = -jnp.exp(alogt_ref[0]) * _softplus(abt_ref[0] + dtbt_ref[0])
    ri = lax.broadcasted_iota(jnp.int32, (T, T), 0)
    ci = lax.broadcasted_iota(jnp.int32, (T, T), 1)
    same = (ri // C) == (ci // C)
    causal = same & (ci <= ri)
    strict = same & (ci < ri)
    eye = (ri == ci).astype(F32)
    g_nat = _dot_exact_rhs(causal, la)
    g_t = _dot_exact_lhs(lat, same & (ri <= ci))
    gend_nat = jnp.concatenate(
        [jnp.broadcast_to(g_nat[c * C + C - 1:c * C + C, :], (C, LANES)) for c in range(T // C)], axis=0)

    low_half = lax.broadcasted_iota(jnp.int32, (1, LANES), 1) < DN_DIM
    high_half = jnp.logical_not(low_half)
    bdr = lax.broadcasted_iota(jnp.int32, (LANES, LANES), 0) // DN_DIM
    bdc = lax.broadcasted_iota(jnp.int32, (LANES, LANES), 1) // DN_DIM
    blockdiag = bdr == bdc

    def per_head_lanes(src, first):
        return jnp.where(low_half, src[:, first:first + 1], src[:, first + 1:first + 2])

    npair = nh // 2
    halves = (low_half, high_half)
    qd, kt, dec_end, rhs, pw, xinv, intras = [], [], [], [], [], [], []
    for p in range(npair):
        sl = slice(p * LANES, (p + 1) * LANES)
        gx = per_head_lanes(g_nat, 2 * p)
        gendx = per_head_lanes(gend_nat, 2 * p)
        bx = per_head_lanes(beta, nh + 2 * p)
        eg = jnp.exp(gx)
        k_p = k[:, sl]
        q_p = q[:, sl]
        kb = k_p * bx
        vb = v[:, sl] * bx
        kbe = kb * eg
        qd.append((q_p * eg).astype(BF16))
        kt.append((k_p * jnp.exp(gendx - gx)).astype(BF16))
        dec_end.append(jnp.exp(gendx))
        k_pb = k_p.astype(BF16)
        for half in range(2):
            h = 2 * p + half
            mh = halves[half]
            diff = g_nat[:, h:h + 1] - g_t[h:h + 1, :]
            decay = jnp.exp(jnp.where(causal, diff, NEG_INF))
            kk = _dot_nt(jnp.where(mh, kb, 0.0).astype(BF16), k_pb)
            qkm = _dot_nt(jnp.where(mh, q_p, 0.0).astype(BF16), k_pb)
            intras.append((qkm * decay).astype(BF16))
            n1 = jnp.where(strict, -(kk * decay), 0.0).astype(BF16)
            pw.append(n1)
            xinv.append(eye + n1.astype(F32))
            rhs.append(jnp.concatenate([jnp.where(mh, vb, 0.0), jnp.where(mh, kbe, 0.0)], axis=1))
    n1s = list(pw)
    for _ in range(5):
        for h in range(nh):
            pw[h] = _dot(pw[h], pw[h]).astype(BF16)
            xinv[h] = xinv[h] + _dot(xinv[h].astype(BF16), pw[h])
    xb = [xinv[h].astype(BF16) for h in range(nh)]
    one = bd[0:1, 0:1].astype(F32)
    for h in range(nh):
        rh, rl = _split_bf16(rhs[h])
        y_ref[h] = _dot(xb[h], rh) + _dot(xb[h], rl)
    for h in range(nh):
        yh, yl = _split_bf16(y_ref[h])
        c_ref[h] = _dot(n1s[h], yh) + _dot(n1s[h], yl)
    for h in range(nh):
        r_ref[h] = ((rhs[h] - y_ref[h]) + c_ref[h]).astype(BF16)
    for h in range(nh):
        c_ref[h] = _dot(xb[h], r_ref[h])
    for h in range(nh):
        y_ref[h] = y_ref[h] + one * c_ref[h]
    u, w_ = [], []
    for p in range(npair):
        u.append(y_ref[2 * p, :, :LANES] + y_ref[2 * p + 1, :, :LANES])
        w_.append((y_ref[2 * p, :, LANES:] + y_ref[2 * p + 1, :, LANES:]).astype(BF16))
    state = [state_ref[p] for p in range(npair)]
    vn = [[] for _ in range(npair)]
    oi = [[] for _ in range(npair)]
    for c in range(T // C):
        rows = slice(c * C, (c + 1) * C)
        for p in range(npair):
            sb = state[p].astype(BF16)
            vn_c = u[p][rows] - _dot(w_[p][rows], sb)
            oi[p].append(_dot(qd[p][rows], sb))
            upd = _dot_tn(kt[p][rows], vn_c.astype(BF16))
            state[p] = state[p] * dec_end[p][c * C:c * C + 1] + jnp.where(blockdiag, upd, 0.0)
            vn[p].append(vn_c)
    outs = []
    for p in range(npair):
        state_ref[p] = state[p]
        vn_all = jnp.concatenate(vn[p], axis=0)
        o_p = jnp.concatenate(oi[p], axis=0)
        for half in range(2):
            o_p = o_p + _dot(intras[2 * p + half], jnp.where(halves[half], vn_all, 0.0).astype(BF16))
        outs.append(o_p)
    o = jnp.concatenate(outs, axis=1)
    ms = _seg_sum(o * o, bd) * (1.0 / DN_DIM)
    zg = _silu(z_ref[0].astype(F32))
    o_ref[0] = (o * lax.rsqrt(ms + EPS) * onorm_ref[0] * zg).astype(BF16)


def _dn_call(proj3, ab, abt, layer, conv_w, alog, dtb, alog_t, dtb_t, onorm, bd):
    T = SEQ_TILE
    tail_rows = 2 * SUBLANES
    per_tile = T // tail_rows
    return pl.pallas_call(
        _dn_kernel,
        grid=(BATCH, SEQ // T),
        in_specs=[
            pl.BlockSpec((1, T, 1536), lambda b, t: (b, t, P_DQKV // 1536)),
            pl.BlockSpec((1, tail_rows, 1536), lambda b, t: (b, jnp.maximum(t * per_tile - 1, 0), P_DQKV // 1536)),
            pl.BlockSpec((1, T, 512), lambda b, t: (b, t, P_Z // 512)),
            pl.BlockSpec((T, LANES), lambda b, t: (b * (SEQ // T) + t, 0)),
            pl.BlockSpec((1, 2 * DN_HEADS, T), lambda b, t: (b, 0, t)),
            _layer_resident((DN_CONV, 1536), layer),
            _layer_resident((1, LANES), layer),
            _layer_resident((1, LANES), layer),
            _layer_resident((2 * DN_HEADS, T), layer),
            _layer_resident((2 * DN_HEADS, T), layer),
            _layer_resident((1, 512), layer),
            _resident((MXU_TILE, MXU_TILE)),
        ],
        out_specs=pl.BlockSpec((1, T, 512), lambda b, t: (b, t, 0)),
        out_shape=jax.ShapeDtypeStruct((BATCH, SEQ, 512), BF16),
        scratch_shapes=[pltpu.VMEM((DN_HEADS // 2, LANES, LANES), F32),
                        pltpu.VMEM((DN_HEADS, T, 2 * LANES), F32),
                        pltpu.VMEM((DN_HEADS, T, 2 * LANES), F32),
                        pltpu.VMEM((DN_HEADS, T, 2 * LANES), BF16)],
        compiler_params=_cparams("parallel", "arbitrary"),
        name="gated_deltanet",
    )(proj3, proj3, proj3, ab, abt, conv_w, alog, dtb, alog_t, dtb_t, onorm, bd)


def _ret_kernel(x_ref, tc_ref, ts_ref, perm_ref, o_ref, state_ref):
    t = pl.program_id(1)
    T = SEQ_TILE

    @pl.when(t == 0)
    def _():
        state_ref[...] = jnp.zeros_like(state_ref)

    x = x_ref[0]
    qk = _rotary(x[:, :512].astype(F32), perm_ref[...], tc_ref[...], ts_ref[...])
    rq = qk[:, :256]
    rk = qk[:, 256:] * (RET_KEY_DIM ** -0.5)
    ri = lax.broadcasted_iota(jnp.int32, (T, T), 0)
    ci = lax.broadcasted_iota(jnp.int32, (T, T), 1)
    causal = ci <= ri
    dist = (ri - ci).astype(F32)
    row = lax.broadcasted_iota(jnp.int32, (T, LANES), 0).astype(F32)
    low_half = lax.broadcasted_iota(jnp.int32, (1, LANES), 1) < RET_KEY_DIM

    for h in range(RET_HEADS):
        log_gamma = math.log1p(-(2.0 ** (-5.0 - h)))
        p = h // 2
        mh = low_half if h % 2 == 0 else jnp.logical_not(low_half)
        q_p = rq[:, p * LANES:(p + 1) * LANES]
        k_p = rk[:, p * LANES:(p + 1) * LANES]
        v_h = x[:, 512 + h * LANES:512 + (h + 1) * LANES]
        decay = jnp.exp(jnp.where(causal, dist * log_gamma, NEG_INF))
        scores = _dot_nt(jnp.where(mh, q_p, 0.0).astype(BF16), k_p.astype(BF16)) * decay
        state = state_ref[h]
        q_dec = jnp.where(mh, q_p * jnp.exp(log_gamma * (row + 1.0)), 0.0)
        out = _dot(scores.astype(BF16), v_h) + _dot(q_dec.astype(BF16), state.astype(BF16))
        k_end = jnp.where(mh, k_p * jnp.exp(log_gamma * (T - 1.0 - row)), 0.0)
        state_ref[h] = state * math.exp(log_gamma * T) + _dot_tn(k_end.astype(BF16), v_h)
        ms = jnp.mean(out * out, axis=-1, keepdims=True)
        g = _silu(x[:, 1024 + h * LANES:1024 + (h + 1) * LANES].astype(F32))
        o_ref[0, :, h * LANES:(h + 1) * LANES] = (out * lax.rsqrt(ms + EPS) * g).astype(BF16)


def _ret_call(proj3, tabs, perm):
    T = SEQ_TILE
    tab = pl.BlockSpec((T, LANES), lambda b, t: (t, 0))
    return pl.pallas_call(
        _ret_kernel,
        grid=(BATCH, SEQ // T),
        in_specs=[pl.BlockSpec((1, T, 1536), lambda b, t: (b, t, P_RET // 1536)), tab, tab,
                  _resident((MXU_TILE, MXU_TILE))],
        out_specs=pl.BlockSpec((1, T, 512), lambda b, t: (b, t, 0)),
        out_shape=jax.ShapeDtypeStruct((BATCH, SEQ, 512), BF16),
        scratch_shapes=[pltpu.VMEM((RET_HEADS, LANES, LANES), F32)],
        compiler_params=_cparams("parallel", "arbitrary"),
        name="retention",
    )(proj3, tabs[0], tabs[1], perm)


def _merge_kernel(x_ref, mod_ref, a_ref, b_ref, c_ref, g_ref, wb_ref, wo_ref, o_ref):
    merged = jnp.zeros((TM, D_MODEL), F32)
    for i, br in enumerate((a_ref, b_ref, c_ref)):
        gate = _sigmoid(g_ref[:, i * D_MODEL:(i + 1) * D_MODEL].astype(F32))
        merged = merged + gate * _dot(br[...], wb_ref[0, i])
    y = _dot(merged.astype(BF16), wo_ref[0])
    o_ref[...] = x_ref[...] + mod_ref[0, 0, 0][2:3] * y


def _merge_call(x2d, mod5, layer, out_a, out_b, out_c, proj2, w_branch, w_out):
    tiles_per_batch = SEQ // TM
    branch = pl.BlockSpec((TM, 512), lambda i: (i, 0))
    return pl.pallas_call(
        _merge_kernel,
        grid=(N_TOK // TM,),
        in_specs=[
            pl.BlockSpec((TM, D_MODEL), lambda i: (i, 0)),
            _mod_spec(layer, 1, lambda i: i // tiles_per_batch),
            branch, branch, branch,
            pl.BlockSpec((TM, 3 * D_MODEL), lambda i: (i, P_GATES // (3 * D_MODEL))),
            _layer_resident((3, 512, D_MODEL), layer),
            _layer_resident((D_MODEL, D_MODEL), layer),
        ],
        out_specs=pl.BlockSpec((TM, D_MODEL), lambda i: (i, 0)),
        out_shape=jax.ShapeDtypeStruct((N_TOK, D_MODEL), F32),
        compiler_params=_cparams("parallel"),
        name="branch_merge",
    )(x2d, mod5, out_a, out_b, out_c, proj2, w_branch, w_out)


def _block_diag_ones(n, seg):
    idx = np.arange(n) // seg
    return jnp.asarray((idx[:, None] == idx[None, :]).astype(np.float32), dtype=BF16)


def _partner_matrix(partner_of):
    m = np.zeros((MXU_TILE, MXU_TILE), np.float32)
    for dst in range(MXU_TILE):
        src = partner_of(dst)
        if src is not None:
            m[src, dst] = 1.0
    return jnp.asarray(m, dtype=BF16)


def _rope_partner(dst):
    d = dst % ATTN_HEAD_DIM
    half = ROPE_DIM // 2
    if d < half:
        return dst + half
    return dst - half if d < ROPE_DIM else None


def _attn_tables():
    half = ROPE_DIM // 2
    pos = jnp.arange(SEQ, dtype=F32)
    inv_freq = ROPE_THETA ** (-jnp.arange(0, ROPE_DIM, 2, dtype=F32) / ROPE_DIM)
    phase = pos[:, None] * inv_freq[None, :]
    cos, sin = jnp.cos(phase), jnp.sin(phase)
    pad = ATTN_HEAD_DIM - ROPE_DIM
    c = jnp.concatenate([cos, cos, jnp.ones((SEQ, pad), F32)], axis=1)
    s = jnp.concatenate([-sin, sin, jnp.zeros((SEQ, pad), F32)], axis=1)
    return jnp.tile(c, (1, 2)), jnp.tile(s, (1, 2))


def _ret_tables():
    pos = jnp.arange(SEQ, dtype=F32)
    angle = 1.0 / (RET_THETA ** jnp.linspace(0.0, 1.0, RET_KEY_DIM // 2, dtype=F32))
    angle = jnp.repeat(angle, 2)
    phase = pos[:, None] * angle[None, :]
    cos, sin = jnp.cos(phase), jnp.sin(phase)
    even = (jnp.arange(RET_KEY_DIM) % 2 == 0)[None, :]
    return jnp.tile(cos, (1, 2)), jnp.tile(jnp.where(even, -sin, sin), (1, 2))


def _relayout_w_in(w):
    cs = lambda a, b: w[:, :, a:b]
    pad = jnp.zeros((DEPTH, D_MODEL, P_ATTN - P_Z - 512), F32)
    cols = [cs(4368, 7440), cs(2832, 4368), cs(768, 2304), cs(2320, 2832), pad, cs(0, 768)]
    w_p = jnp.concatenate(cols, axis=2).astype(BF16)
    ab = cs(2304, 2320)
    w_ab = jnp.concatenate([ab, jnp.zeros((DEPTH, D_MODEL, LANES - 2 * DN_HEADS), F32)], axis=2).astype(BF16)
    return w_p, w_ab, jnp.transpose(ab, (0, 2, 1)).astype(BF16)


def kernel(x, c, w_mod, b_mod, ffn1_norm, ffn1_w13, ffn1_w2, mix_norm, w_in, attn_q_norm, attn_k_norm,
           attn_sinks, dn_conv, dn_a_log, dn_dt_bias, dn_out_norm, w_branch, w_out, ffn2_norm, ffn2_w13,
           ffn2_w2):
    mod5 = _mod_call(c, w_mod, b_mod).reshape(DEPTH, BATCH, 3, 3, D_MODEL)
    attn_tabs = _attn_tables()
    ret_tabs = _ret_tables()
    bd = _block_diag_ones(MXU_TILE, ATTN_HEAD_DIM)
    rope_perm = _partner_matrix(_rope_partner)
    ret_perm = _partner_matrix(lambda dst: dst + 1 if dst % 2 == 0 else dst - 1)

    w13_1, w2_1 = ffn1_w13.astype(BF16), ffn1_w2.astype(BF16)
    w13_2, w2_2 = ffn2_w13.astype(BF16), ffn2_w2.astype(BF16)
    w_in_p, w_ab, w_abt = _relayout_w_in(w_in)
    w_branch_b, w_out_b = w_branch.astype(BF16), w_out.astype(BF16)
    gain1, gain_mix, gain2 = (g.reshape(DEPTH, 1, D_MODEL) for g in (ffn1_norm, mix_norm, ffn2_norm))
    qk_gain = jnp.concatenate([jnp.tile(attn_q_norm, (1, ATTN_Q_HEADS)),
                               jnp.tile(attn_k_norm, (1, ATTN_KV_HEADS))], axis=1)[:, None, :]
    sinks = jnp.broadcast_to(attn_sinks[:, :, None], (DEPTH, ATTN_Q_HEADS, LANES))
    lane_pad = ((0, 0), (0, LANES - DN_HEADS))
    alog = jnp.pad(dn_a_log, lane_pad)[:, None, :]
    dtb = jnp.pad(dn_dt_bias, lane_pad)[:, None, :]
    row_pad = ((0, 0), (0, DN_HEADS))
    alog_t = jnp.broadcast_to(jnp.pad(dn_a_log, row_pad)[:, :, None], (DEPTH, 2 * DN_HEADS, SEQ_TILE))
    dtb_t = jnp.broadcast_to(jnp.pad(dn_dt_bias, row_pad)[:, :, None], (DEPTH, 2 * DN_HEADS, SEQ_TILE))
    onorm = jnp.tile(dn_out_norm, (1, DN_HEADS))[:, None, :]

    xf = x.reshape(N_TOK, D_MODEL)
    for l in range(DEPTH):
        xf = _ffn_call(xf, mod5, l, 0, gain1, w13_1, w2_1)
        proj, ab, abt = _inproj_call(xf, mod5, l, gain_mix, w_in_p, w_ab, w_abt)
        proj3 = proj.reshape(BATCH, SEQ, N_PROJ)
        out_a = _attn_call(proj3, l, attn_tabs, qk_gain, bd, rope_perm, sinks)
        out_b = _dn_call(proj3, ab, abt, l, dn_conv, alog, dtb, alog_t, dtb_t, onorm, bd)
        out_c = _ret_call(proj3, ret_tabs, ret_perm)
        xf = _merge_call(xf, mod5, l, out_a.reshape(N_TOK, 512), out_b.reshape(N_TOK, 512),
                         out_c.reshape(N_TOK, 512), proj, w_branch_b, w_out_b)
        xf = _ffn_call(xf, mod5, l, 2, gain2, w13_2, w2_2)
    return xf.reshape(BATCH, SEQ, D_MODEL)
```

```python
import math

import numpy as np
import jax
import jax.numpy as jnp
from jax import lax
from jax.experimental import pallas as pl
from jax.experimental.pallas import tpu as pltpu

F32 = jnp.float32
BF16 = jnp.bfloat16
HIGHEST = lax.Precision.HIGHEST

D_MODEL = 1024
BATCH = 8
SEQ = 2048
DEPTH = 2
N_TOK = BATCH * SEQ
N_MOD = 9
D_FF = 2816
EPS = 1e-6
NEG_INF = -1e30

ATTN_Q_HEADS = 8
ATTN_KV_HEADS = 2
ATTN_HEAD_DIM = 64
ATTN_BLOCK = 128
ROPE_DIM = 16
ROPE_THETA = 500000.0
DN_HEADS = 8
DN_DIM = 64
DN_CONV = 4
RET_HEADS = 4
RET_KEY_DIM = 64
RET_THETA = 10000.0

LANES = 128
SUBLANES = 8
MXU_TILE = 256

P_GATES = 0
P_RET = 3072
P_DQKV = 4608
P_Z = 6144
P_ATTN = 6912
N_PROJ = 7680
ATTN_W = 768
ATTN_QK = 640

TM = 512
FFN_CHUNK = 256
N_FFN_CHUNK = D_FF // FFN_CHUNK
PROJ_CHUNK = 768
MOD_TN = 1152
SEQ_TILE = 256
ATTN_TILE = 512
DN_CHUNK = 64
VMEM_LIMIT = 56 * 1024 * 1024


def _dot(a, b, precision=None):
    return jnp.dot(a, b, preferred_element_type=F32, precision=precision)


def _dot_nt(a, b):
    return lax.dot_general(a, b, (((1,), (1,)), ((), ())), preferred_element_type=F32)


def _dot_tn(a, b):
    return lax.dot_general(a, b, (((0,), (0,)), ((), ())), preferred_element_type=F32)


def _split_bf16(a):
    hi = a.astype(BF16)
    return hi, (a - hi.astype(F32)).astype(BF16)


def _split3_bf16(a):
    h1 = a.astype(BF16)
    r1 = a - h1.astype(F32)
    h2 = r1.astype(BF16)
    h3 = (r1 - h2.astype(F32)).astype(BF16)
    return h1, h2, h3


def _dot_exact_rhs(m01, x):
    m = m01.astype(BF16)
    h1, h2, h3 = _split3_bf16(x)
    return _dot(m, h1) + (_dot(m, h2) + _dot(m, h3))


def _dot_exact_lhs(x, m01):
    m = m01.astype(BF16)
    h1, h2, h3 = _split3_bf16(x)
    return _dot(h1, m) + (_dot(h2, m) + _dot(h3, m))


def _sigmoid(x):
    return 0.5 * (jnp.tanh(0.5 * x) + 1.0)


def _silu(x):
    return x * _sigmoid(x)


def _modulate(x, gain, shift, scale):
    ms = jnp.mean(x * x, axis=-1, keepdims=True)
    return x * lax.rsqrt(ms + EPS) * (gain * (1.0 + scale)) + shift


def _lane_block_matmul(pieces, m):
    blk = m.shape[0]
    width = pieces[0].shape[-1]
    outs = []
    for s in range(0, width, blk):
        n = min(blk, width - s)
        mm = m if n == blk else m[:n, :n]
        acc = _dot(pieces[0][:, s:s + n], mm)
        for piece in pieces[1:]:
            acc = acc + _dot(piece[:, s:s + n], mm)
        outs.append(acc)
    return outs[0] if len(outs) == 1 else jnp.concatenate(outs, axis=1)


def _seg_sum(x2, bd):
    return _lane_block_matmul([x2.astype(BF16)], bd)


def _tile_lanes(t, width):
    reps = width // t.shape[-1]
    return t if reps == 1 else jnp.concatenate([t] * reps, axis=1)


def _rotary(x, perm, cos, sin):
    w = x.shape[-1]
    return x * _tile_lanes(cos, w) + _lane_block_matmul(list(_split_bf16(x)), perm) * _tile_lanes(sin, w)


def _cparams(*sem):
    return pltpu.CompilerParams(dimension_semantics=sem, vmem_limit_bytes=VMEM_LIMIT)


def _resident(shape):
    zeros = (0,) * len(shape)
    return pl.BlockSpec(shape, lambda *_: zeros, pipeline_mode=pl.Buffered(1))


def _layer_resident(shape, layer):
    idx = (layer,) + (0,) * len(shape)
    return pl.BlockSpec((1,) + tuple(shape), lambda *_: idx, pipeline_mode=pl.Buffered(1))


def _mod_spec(layer, group, batch_of):
    return pl.BlockSpec((1, 1, 1, 3, D_MODEL), lambda *g: (layer, batch_of(*g), group, 0, 0))


def _mod_kernel(c_ref, w_ref, b_ref, o_ref):
    c = c_ref[...]
    o_ref[0] = _dot(_silu(c), w_ref[0], precision=HIGHEST) + b_ref[0]


def _mod_call(c, w_mod, b_mod):
    n = N_MOD * D_MODEL
    return pl.pallas_call(
        _mod_kernel,
        grid=(DEPTH, n // MOD_TN),
        in_specs=[
            pl.BlockSpec((BATCH, D_MODEL), lambda l, j: (0, 0)),
            pl.BlockSpec((1, D_MODEL, MOD_TN), lambda l, j: (l, 0, j)),
            pl.BlockSpec((1, 1, MOD_TN), lambda l, j: (l, 0, j)),
        ],
        out_specs=pl.BlockSpec((1, BATCH, MOD_TN), lambda l, j: (l, 0, j)),
        out_shape=jax.ShapeDtypeStruct((DEPTH, BATCH, n), F32),
        compiler_params=_cparams("parallel", "parallel"),
        name="adaln_mod",
    )(c, w_mod, b_mod.reshape(DEPTH, 1, n))


def _ffn_kernel(x_ref, mod_ref, gain_ref, w13_ref, w2_ref, o_ref, h_ref):
    x = x_ref[...]
    mod = mod_ref[0, 0, 0]
    ub = _modulate(x, gain_ref[0], mod[0:1], mod[1:2]).astype(BF16)
    for j in range(N_FFN_CHUNK):
        lo = j * FFN_CHUNK
        g = _dot(ub, w13_ref[0, :, lo:lo + FFN_CHUNK])
        up = _dot(ub, w13_ref[0, :, D_FF + lo:D_FF + lo + FFN_CHUNK])
        h_ref[:, lo:lo + FFN_CHUNK] = (_silu(g) * up).astype(BF16)
    y = _dot(h_ref[...], w2_ref[0])
    o_ref[...] = x + (0.5 * mod[2:3]) * y


def _ffn_call(x2d, mod5, layer, group, gains, w13, w2):
    tiles_per_batch = SEQ // TM
    return pl.pallas_call(
        _ffn_kernel,
        grid=(N_TOK // TM,),
        in_specs=[
            pl.BlockSpec((TM, D_MODEL), lambda i: (i, 0)),
            _mod_spec(layer, group, lambda i: i // tiles_per_batch),
            _layer_resident((1, D_MODEL), layer),
            _layer_resident((D_MODEL, 2 * D_FF), layer),
            _layer_resident((D_FF, D_MODEL), layer),
        ],
        out_specs=pl.BlockSpec((TM, D_MODEL), lambda i: (i, 0)),
        out_shape=jax.ShapeDtypeStruct((N_TOK, D_MODEL), F32),
        scratch_shapes=[pltpu.VMEM((TM, D_FF), BF16)],
        compiler_params=_cparams("parallel"),
        name="swiglu_ffn",
    )(x2d, mod5, gains, w13, w2)


def _inproj_kernel(x_ref, mod_ref, gain_ref, w_ref, wab_ref, wabt_ref, proj_ref, ab_ref, abt_ref):
    mod = mod_ref[0, 0, 0]
    ub = _modulate(x_ref[...], gain_ref[0], mod[0:1], mod[1:2]).astype(BF16)
    for j in range(N_PROJ // PROJ_CHUNK):
        sl = slice(j * PROJ_CHUNK, (j + 1) * PROJ_CHUNK)
        proj_ref[:, sl] = _dot(ub, w_ref[0, :, sl]).astype(BF16)
    ab_ref[...] = _dot(ub, wab_ref[0])
    abt_ref[0] = _dot_nt(wabt_ref[0], ub)


def _inproj_call(x2d, mod5, layer, gains, w_in_p, w_ab, w_abt):
    tiles_per_batch = SEQ // TM
    return pl.pallas_call(
        _inproj_kernel,
        grid=(N_TOK // TM,),
        in_specs=[
            pl.BlockSpec((TM, D_MODEL), lambda i: (i, 0)),
            _mod_spec(layer, 1, lambda i: i // tiles_per_batch),
            _layer_resident((1, D_MODEL), layer),
            _layer_resident((D_MODEL, N_PROJ), layer),
            _layer_resident((D_MODEL, LANES), layer),
            _layer_resident((2 * DN_HEADS, D_MODEL), layer),
        ],
        out_specs=[
            pl.BlockSpec((TM, N_PROJ), lambda i: (i, 0)),
            pl.BlockSpec((TM, LANES), lambda i: (i, 0)),
            pl.BlockSpec((1, 2 * DN_HEADS, TM), lambda i: (i // tiles_per_batch, 0, i % tiles_per_batch)),
        ],
        out_shape=[
            jax.ShapeDtypeStruct((N_TOK, N_PROJ), BF16),
            jax.ShapeDtypeStruct((N_TOK, LANES), F32),
            jax.ShapeDtypeStruct((BATCH, 2 * DN_HEADS, SEQ), F32),
        ],
        compiler_params=_cparams("parallel"),
        name="mixer_in_proj",
    )(x2d, mod5, gains, w_in_p, w_ab, w_abt)


def _attn_kernel(cur_ref, prev_ref, tc_ref, ts_ref, pc_ref, ps_ref, gain_ref, bd_ref, perm_ref,
                 sink_ref, o_ref):
    t = pl.program_id(1)
    blk = ATTN_BLOCK
    nblk = ATTN_TILE // blk
    per_group = ATTN_Q_HEADS // ATTN_KV_HEADS
    bd = bd_ref[...]
    perm = perm_ref[...]
    gain = gain_ref[0]

    def norm_rope(x, g, cos, sin):
        x = x * lax.rsqrt(_seg_sum(x * x, bd) * (1.0 / ATTN_HEAD_DIM) + EPS) * g
        return _rotary(x, perm, cos, sin)

    cur = cur_ref[0]
    prv = prev_ref[0]
    qk = norm_rope(cur[:, :ATTN_QK].astype(F32), gain, tc_ref[...], ts_ref[...])
    kprev = norm_rope(prv[:, 512:ATTN_QK].astype(F32), gain[:, 512:], pc_ref[...], ps_ref[...])
    q = qk[:, :512] * (ATTN_HEAD_DIM ** -0.5)
    k_all = jnp.concatenate([kprev, qk[:, 512:]], axis=0)
    v_all = jnp.concatenate([prv[:, ATTN_QK:], cur[:, ATTN_QK:]], axis=0).astype(F32)

    low_half = lax.broadcasted_iota(jnp.int32, (1, LANES), 1) < ATTN_HEAD_DIM

    def group_on_both_halves(a, g):
        swapped = pltpu.roll(a, ATTN_HEAD_DIM, 1)
        return (jnp.where(low_half, a, swapped) if g == 0 else jnp.where(low_half, swapped, a)).astype(BF16)

    kd = [group_on_both_halves(k_all, g) for g in range(ATTN_KV_HEADS)]
    vd = [group_on_both_halves(v_all, g) for g in range(ATTN_KV_HEADS)]

    kj = lax.broadcasted_iota(jnp.int32, (2 * blk, blk), 0)
    qi = lax.broadcasted_iota(jnp.int32, (2 * blk, blk), 1)
    in_window = (kj > qi) & (kj <= qi + blk)
    bias = jnp.where(in_window, 0.0, NEG_INF)
    bias_no_prev = jnp.where(in_window & (kj >= blk), 0.0, NEG_INF)
    bias_rest = jnp.concatenate([bias] * per_group, axis=1)
    bias_first = jnp.where(t == 0, jnp.concatenate([bias_no_prev] * per_group, axis=1), bias_rest)
    sink_row = lax.broadcasted_iota(jnp.int32, (2 * blk, per_group * blk), 0) == 0
    sinks = sink_ref[0]
    ones_blk = jnp.ones((2 * blk, LANES), BF16)
    first_key = lax.broadcasted_iota(jnp.int32, (2 * blk, LANES), 0) == 0

    units = [(i, g) for i in range(nblk) for g in range(ATTN_KV_HEADS)]
    logits = []
    for i, g in units:
        rows = slice(i * blk, (i + 1) * blk)
        parts = []
        for p in range(g * per_group // 2, (g + 1) * per_group // 2):
            qp = q[rows, p * LANES:(p + 1) * LANES]
            parts += [jnp.where(low_half, qp, 0.0), jnp.where(low_half, 0.0, qp)]
        qs = jnp.concatenate(parts, axis=0).astype(BF16)
        band = kd[g][i * blk:(i + 2) * blk]
        sink = jnp.concatenate(
            [jnp.broadcast_to(sinks[g * per_group + r:g * per_group + r + 1, :], (2 * blk, LANES))
             for r in range(per_group)], axis=1)
        lg = _dot_nt(band, qs) + (bias_first if i == 0 else bias_rest)
        logits.append(jnp.where(sink_row, sink, lg))
    expd = []
    for lg in logits:
        m = jnp.max(lg, axis=0, keepdims=True)
        expd.append(jnp.exp(lg - m).astype(BF16))
    for (i, g), e in zip(units, expd):
        vals = jnp.where(first_key, 0.0, vd[g][i * blk:(i + 2) * blk].astype(F32)).astype(BF16)
        o = _dot_tn(e, jnp.concatenate([vals, ones_blk], axis=1))
        o = o[:, :LANES] * (1.0 / o[:, LANES:])
        for j in range(per_group // 2):
            p = g * per_group // 2 + j
            pair = jnp.where(low_half, o[2 * j * blk:(2 * j + 1) * blk], o[(2 * j + 1) * blk:(2 * j + 2) * blk])
            o_ref[0, i * blk:(i + 1) * blk, p * LANES:(p + 1) * LANES] = pair.astype(BF16)


def _attn_call(proj3, layer, tabs, gains, bd, perm, sinks):
    nblk = ATTN_TILE // ATTN_BLOCK
    col = P_ATTN // ATTN_W
    tab_cur = pl.BlockSpec((ATTN_TILE, LANES), lambda b, t: (t, 0))
    tab_prev = pl.BlockSpec((ATTN_BLOCK, LANES), lambda b, t: (jnp.maximum(t * nblk - 1, 0), 0))
    return pl.pallas_call(
        _attn_kernel,
        grid=(BATCH, SEQ // ATTN_TILE),
        in_specs=[
            pl.BlockSpec((1, ATTN_TILE, ATTN_W), lambda b, t: (b, t, col)),
            pl.BlockSpec((1, ATTN_BLOCK, ATTN_W), lambda b, t: (b, jnp.maximum(t * nblk - 1, 0), col)),
            tab_cur, tab_cur, tab_prev, tab_prev,
            _layer_resident((1, ATTN_QK), layer),
            _resident((MXU_TILE, MXU_TILE)),
            _resident((MXU_TILE, MXU_TILE)),
            _layer_resident((ATTN_Q_HEADS, LANES), layer),
        ],
        out_specs=pl.BlockSpec((1, ATTN_TILE, 512), lambda b, t: (b, t, 0)),
        out_shape=jax.ShapeDtypeStruct((BATCH, SEQ, 512), BF16),
        compiler_params=_cparams("parallel", "parallel"),
        name="swa_attention",
    )(proj3, proj3, tabs[0], tabs[1], tabs[0], tabs[1], gains, bd, perm, sinks)


def _softplus(x):
    return jnp.maximum(x, 0.0) + jnp.log(1.0 + jnp.exp(-jnp.abs(x)))


def _dn_kernel(cur_ref, prev_ref, z_ref, ab_ref, abt_ref, conv_ref, alog_ref, dtb_ref,
               alogt_ref, dtbt_ref, onorm_ref, bd_ref, o_ref, state_ref, y_ref, c_ref, r_ref):
    t = pl.program_id(1)
    T = SEQ_TILE
    C = DN_CHUNK
    nh = DN_HEADS
    width = nh * DN_DIM

    @pl.when(t == 0)
    def _():
        state_ref[...] = jnp.zeros_like(state_ref)

    x = cur_ref[0].astype(F32)
    tail = jnp.where(t > 0, prev_ref[0].astype(F32)[SUBLANES:], 0.0)
    xc = jnp.concatenate([tail, x], axis=0)
    w = conv_ref[0]
    y = xc[SUBLANES:SUBLANES + T] * w[3:4]
    for j in range(DN_CONV - 1):
        off = SUBLANES - (DN_CONV - 1) + j
        y = y + xc[off:off + T] * w[j:j + 1]
    y = _silu(y)
    bd = bd_ref[...]
    q = y[:, :width]
    k = y[:, width:2 * width]
    v = y[:, 2 * width:]
    q = q * lax.rsqrt(_seg_sum(q * q, bd) + EPS) * (DN_DIM ** -0.5)
    k = k * lax.rsqrt(_seg_sum(k * k, bd) + EPS)

    ab = ab_ref[...]
    la = -jnp.exp(alog_ref[0]) * _softplus(ab + dtb_ref[0])
    beta = _sigmoid(ab)
    lat = -jnp.exp(alogt_ref[0]) * _softplus(abt_ref[0] + dtbt_ref[0])
    ri = lax.broadcasted_iota(jnp.int32, (T, T), 0)
    ci = lax.broadcasted_iota(jnp.int32, (T, T), 1)
    same = (ri // C) == (ci // C)
    causal = same & (ci <= ri)
    strict = same & (ci < ri)
    eye = (ri == ci).astype(F32)
    g_nat = _dot_exact_rhs(causal, la)
    g_t = _dot_exact_lhs(lat, same & (ri <= ci))
    gend_nat = jnp.concatenate(
        [jnp.broadcast_to(g_nat[c * C + C - 1:c * C + C, :], (C, LANES)) for c in range(T // C)], axis=0)

    low_half = lax.broadcasted_iota(jnp.int32, (1, LANES), 1) < DN_DIM
    high_half = jnp.logical_not(low_half)
    bdr = lax.broadcasted_iota(jnp.int32, (LANES, LANES), 0) // DN_DIM
    bdc = lax.broadcasted_iota(jnp.int32, (LANES, LANES), 1) // DN_DIM
    blockdiag = bdr == bdc

    def per_head_lanes(src, first):
        return jnp.where(low_half, src[:, first:first + 1], src[:, first + 1:first + 2])

    npair = nh // 2
    halves = (low_half, high_half)
    qd, kt, dec_end, rhs, pw, xinv, intras = [], [], [], [], [], [], []
    for p in range(npair):
        sl = slice(p * LANES, (p + 1) * LANES)
        gx = per_head_lanes(g_nat, 2 * p)
        gendx = per_head_lanes(gend_nat, 2 * p)
        bx = per_head_lanes(beta, nh + 2 * p)
        eg = jnp.exp(gx)
        k_p = k[:, sl]
        q_p = q[:, sl]
        kb = k_p * bx
        vb = v[:, sl] * bx
        kbe = kb * eg
        qd.append((q_p * eg).astype(BF16))
        kt.append((k_p * jnp.exp(gendx - gx)).astype(BF16))
        dec_end.append(jnp.exp(gendx))
        k_pb = k_p.astype(BF16)
        for half in range(2):
            h = 2 * p + half
            mh = halves[half]
            diff = g_nat[:, h:h + 1] - g_t[h:h + 1, :]
            decay = jnp.exp(jnp.where(causal, diff, NEG_INF))
            kk = _dot_nt(jnp.where(mh, kb, 0.0).astype(BF16), k_pb)
            qkm = _dot_nt(jnp.where(mh, q_p, 0.0).astype(BF16), k_pb)
            intras.append((qkm * decay).astype(BF16))
            n1 = jnp.where(strict, -(kk * decay), 0.0).astype(BF16)
            pw.append(n1)
            xinv.append(eye + n1.astype(F32))
            rhs.append(jnp.concatenate([jnp.where(mh, vb, 0.0), jnp.where(mh, kbe, 0.0)], axis=1))
    n1s = list(pw)
    for _ in range(5):
        for h in range(nh):
            pw[h] = _dot(pw[h], pw[h]).astype(BF16)
            xinv[h] = xinv[h] + _dot(xinv[h].astype(BF16), pw[h])
    xb = [xinv[h].astype(BF16) for h in range(nh)]
    one = bd[0:1, 0:1].astype(F32)
    for h in range(nh):
        y_ref[h] = _dot(xb[h], rhs[h].astype(BF16))
    for h in range(nh):
        yh, yl = _split_bf16(y_ref[h])
        c_ref[h] = _dot(n1s[h], yh) + _dot(n1s[h], yl)
    for h in range(nh):
        r_ref[h] = ((rhs[h] - y_ref[h]) + c_ref[h]).astype(BF16)
    for h in range(nh):
        c_ref[h] = _dot(xb[h], r_ref[h])
    for h in range(nh):
        y_ref[h] = y_ref[h] + one * c_ref[h]
    u, w_ = [], []
    for p in range(npair):
        u.append(y_ref[2 * p, :, :LANES] + y_ref[2 * p + 1, :, :LANES])
        w_.append((y_ref[2 * p, :, LANES:] + y_ref[2 * p + 1, :, LANES:]).astype(BF16))
    state = [state_ref[p] for p in range(npair)]
    vn = [[] for _ in range(npair)]
    oi = [[] for _ in range(npair)]
    for c in range(T // C):
        rows = slice(c * C, (c + 1) * C)
        for p in range(npair):
            sb = state[p].astype(BF16)
            vn_c = u[p][rows] - _dot(w_[p][rows], sb)
            oi[p].append(_dot(qd[p][rows], sb))
            upd = _dot_tn(kt[p][rows], vn_c.astype(BF16))
            state[p] = state[p] * dec_end[p][c * C:c * C + 1] + jnp.where(blockdiag, upd, 0.0)
            vn[p].append(vn_c)
    outs = []
    for p in range(npair):
        state_ref[p] = state[p]
        vn_all = jnp.concatenate(vn[p], axis=0)
        o_p = jnp.concatenate(oi[p], axis=0)
        for half in range(2):
            o_p = o_p + _dot(intras[2 * p + half], jnp.where(halves[half], vn_all, 0.0).astype(BF16))
        outs.append(o_p)
    o = jnp.concatenate(outs, axis=1)
    ms = _seg_sum(o * o, bd) * (1.0 / DN_DIM)
    zg = _silu(z_ref[0].astype(F32))
    o_ref[0] = (o * lax.rsqrt(ms + EPS) * onorm_ref[0] * zg).astype(BF16)


def _dn_call(proj3, ab, abt, layer, conv_w, alog, dtb, alog_t, dtb_t, onorm, bd):
    T = SEQ_TILE
    tail_rows = 2 * SUBLANES
    per_tile = T // tail_rows
    return pl.pallas_call(
        _dn_kernel,
        grid=(BATCH, SEQ // T),
        in_specs=[
            pl.BlockSpec((1, T, 1536), lambda b, t: (b, t, P_DQKV // 1536)),
            pl.BlockSpec((1, tail_rows, 1536), lambda b, t: (b, jnp.maximum(t * per_tile - 1, 0), P_DQKV // 1536)),
            pl.BlockSpec((1, T, 512), lambda b, t: (b, t, P_Z // 512)),
            pl.BlockSpec((T, LANES), lambda b, t: (b * (SEQ // T) + t, 0)),
            pl.BlockSpec((1, 2 * DN_HEADS, T), lambda b, t: (b, 0, t)),
            _layer_resident((DN_CONV, 1536), layer),
            _layer_resident((1, LANES), layer),
            _layer_resident((1, LANES), layer),
            _layer_resident((2 * DN_HEADS, T), layer),
            _layer_resident((2 * DN_HEADS, T), layer),
            _layer_resident((1, 512), layer),
            _resident((MXU_TILE, MXU_TILE)),
        ],
        out_specs=pl.BlockSpec((1, T, 512), lambda b, t: (b, t, 0)),
        out_shape=jax.ShapeDtypeStruct((BATCH, SEQ, 512), BF16),
        scratch_shapes=[pltpu.VMEM((DN_HEADS // 2, LANES, LANES), F32),
                        pltpu.VMEM((DN_HEADS, T, 2 * LANES), F32),
                        pltpu.VMEM((DN_HEADS, T, 2 * LANES), F32),
                        pltpu.VMEM((DN_HEADS, T, 2 * LANES), BF16)],
        compiler_params=_cparams("parallel", "arbitrary"),
        name="gated_deltanet",
    )(proj3, proj3, proj3, ab, abt, conv_w, alog, dtb, alog_t, dtb_t, onorm, bd)


def _ret_kernel(x_ref, tc_ref, ts_ref, perm_ref, o_ref, state_ref):
    t = pl.program_id(1)
    T = SEQ_TILE

    @pl.when(t == 0)
    def _():
        state_ref[...] = jnp.zeros_like(state_ref)

    x = x_ref[0]
    qk = _rotary(x[:, :512].astype(F32), perm_ref[...], tc_ref[...], ts_ref[...])
    rq = qk[:, :256]
    rk = qk[:, 256:] * (RET_KEY_DIM ** -0.5)
    ri = lax.broadcasted_iota(jnp.int32, (T, T), 0)
    ci = lax.broadcasted_iota(jnp.int32, (T, T), 1)
    causal = ci <= ri
    dist = (ri - ci).astype(F32)
    row = lax.broadcasted_iota(jnp.int32, (T, LANES), 0).astype(F32)
    low_half = lax.broadcasted_iota(jnp.int32, (1, LANES), 1) < RET_KEY_DIM

    for h in range(RET_HEADS):
        log_gamma = math.log1p(-(2.0 ** (-5.0 - h)))
        p = h // 2
        mh = low_half if h % 2 == 0 else jnp.logical_not(low_half)
        q_p = rq[:, p * LANES:(p + 1) * LANES]
        k_p = rk[:, p * LANES:(p + 1) * LANES]
        v_h = x[:, 512 + h * LANES:512 + (h + 1) * LANES]
        decay = jnp.exp(jnp.where(causal, dist * log_gamma, NEG_INF))
        scores = _dot_nt(jnp.where(mh, q_p, 0.0).astype(BF16), k_p.astype(BF16)) * decay
        state = state_ref[h]
        q_dec = jnp.where(mh, q_p * jnp.exp(log_gamma * (row + 1.0)), 0.0)
        out = _dot(scores.astype(BF16), v_h) + _dot(q_dec.astype(BF16), state.astype(BF16))
        k_end = jnp.where(mh, k_p * jnp.exp(log_gamma * (T - 1.0 - row)), 0.0)
        state_ref[h] = state * math.exp(log_gamma * T) + _dot_tn(k_end.astype(BF16), v_h)
        ms = jnp.mean(out * out, axis=-1, keepdims=True)
        g = _silu(x[:, 1024 + h * LANES:1024 + (h + 1) * LANES].astype(F32))
        o_ref[0, :, h * LANES:(h + 1) * LANES] = (out * lax.rsqrt(ms + EPS) * g).astype(BF16)


def _ret_call(proj3, tabs, perm):
    T = SEQ_TILE
    tab = pl.BlockSpec((T, LANES), lambda b, t: (t, 0))
    return pl.pallas_call(
        _ret_kernel,
        grid=(BATCH, SEQ // T),
        in_specs=[pl.BlockSpec((1, T, 1536), lambda b, t: (b, t, P_RET // 1536)), tab, tab,
                  _resident((MXU_TILE, MXU_TILE))],
        out_specs=pl.BlockSpec((1, T, 512), lambda b, t: (b, t, 0)),
        out_shape=jax.ShapeDtypeStruct((BATCH, SEQ, 512), BF16),
        scratch_shapes=[pltpu.VMEM((RET_HEADS, LANES, LANES), F32)],
        compiler_params=_cparams("parallel", "arbitrary"),
        name="retention",
    )(proj3, tabs[0], tabs[1], perm)


def _merge_kernel(x_ref, mod_ref, a_ref, b_ref, c_ref, g_ref, wb_ref, wo_ref, o_ref):
    merged = jnp.zeros((TM, D_MODEL), F32)
    for i, br in enumerate((a_ref, b_ref, c_ref)):
        gate = _sigmoid(g_ref[:, i * D_MODEL:(i + 1) * D_MODEL].astype(F32))
        merged = merged + gate * _dot(br[...], wb_ref[0, i])
    y = _dot(merged.astype(BF16), wo_ref[0])
    o_ref[...] = x_ref[...] + mod_ref[0, 0, 0][2:3] * y


def _merge_call(x2d, mod5, layer, out_a, out_b, out_c, proj2, w_branch, w_out):
    tiles_per_batch = SEQ // TM
    branch = pl.BlockSpec((TM, 512), lambda i: (i, 0))
    return pl.pallas_call(
        _merge_kernel,
        grid=(N_TOK // TM,),
        in_specs=[
            pl.BlockSpec((TM, D_MODEL), lambda i: (i, 0)),
            _mod_spec(layer, 1, lambda i: i // tiles_per_batch),
            branch, branch, branch,
            pl.BlockSpec((TM, 3 * D_MODEL), lambda i: (i, P_GATES // (3 * D_MODEL))),
            _layer_resident((3, 512, D_MODEL), layer),
            _layer_resident((D_MODEL, D_MODEL), layer),
        ],
        out_specs=pl.BlockSpec((TM, D_MODEL), lambda i: (i, 0)),
        out_shape=jax.ShapeDtypeStruct((N_TOK, D_MODEL), F32),
        compiler_params=_cparams("parallel"),
        name="branch_merge",
    )(x2d, mod5, out_a, out_b, out_c, proj2, w_branch, w_out)


def _block_diag_ones(n, seg):
    idx = np.arange(n) // seg
    return jnp.asarray((idx[:, None] == idx[None, :]).astype(np.float32), dtype=BF16)


def _partner_matrix(partner_of):
    m = np.zeros((MXU_TILE, MXU_TILE), np.float32)
    for dst in range(MXU_TILE):
        src = partner_of(dst)
        if src is not None:
            m[src, dst] = 1.0
    return jnp.asarray(m, dtype=BF16)


def _rope_partner(dst):
    d = dst % ATTN_HEAD_DIM
    half = ROPE_DIM // 2
    if d < half:
        return dst + half
    return dst - half if d < ROPE_DIM else None


def _attn_tables():
    half = ROPE_DIM // 2
    pos = jnp.arange(SEQ, dtype=F32)
    inv_freq = ROPE_THETA ** (-jnp.arange(0, ROPE_DIM, 2, dtype=F32) / ROPE_DIM)
    phase = pos[:, None] * inv_freq[None, :]
    cos, sin = jnp.cos(phase), jnp.sin(phase)
    pad = ATTN_HEAD_DIM - ROPE_DIM
    c = jnp.concatenate([cos, cos, jnp.ones((SEQ, pad), F32)], axis=1)
    s = jnp.concatenate([-sin, sin, jnp.zeros((SEQ, pad), F32)], axis=1)
    return jnp.tile(c, (1, 2)), jnp.tile(s, (1, 2))


def _ret_tables():
    pos = jnp.arange(SEQ, dtype=F32)
    angle = 1.0 / (RET_THETA ** jnp.linspace(0.0, 1.0, RET_KEY_DIM // 2, dtype=F32))
    angle = jnp.repeat(angle, 2)
    phase = pos[:, None] * angle[None, :]
    cos, sin = jnp.cos(phase), jnp.sin(phase)
    even = (jnp.arange(RET_KEY_DIM) % 2 == 0)[None, :]
    return jnp.tile(cos, (1, 2)), jnp.tile(jnp.where(even, -sin, sin), (1, 2))


def _relayout_w_in(w):
    cs = lambda a, b: w[:, :, a:b]
    pad = jnp.zeros((DEPTH, D_MODEL, P_ATTN - P_Z - 512), F32)
    cols = [cs(4368, 7440), cs(2832, 4368), cs(768, 2304), cs(2320, 2832), pad, cs(0, 768)]
    w_p = jnp.concatenate(cols, axis=2).astype(BF16)
    ab = cs(2304, 2320)
    w_ab = jnp.concatenate([ab, jnp.zeros((DEPTH, D_MODEL, LANES - 2 * DN_HEADS), F32)], axis=2).astype(BF16)
    return w_p, w_ab, jnp.transpose(ab, (0, 2, 1)).astype(BF16)


def kernel(x, c, w_mod, b_mod, ffn1_norm, ffn1_w13, ffn1_w2, mix_norm, w_in, attn_q_norm, attn_k_norm,
           attn_sinks, dn_conv, dn_a_log, dn_dt_bias, dn_out_norm, w_branch, w_out, ffn2_norm, ffn2_w13,
           ffn2_w2):
    mod5 = _mod_call(c, w_mod, b_mod).reshape(DEPTH, BATCH, 3, 3, D_MODEL)
    attn_tabs = _attn_tables()
    ret_tabs = _ret_tables()
    bd = _block_diag_ones(MXU_TILE, ATTN_HEAD_DIM)
    rope_perm = _partner_matrix(_rope_partner)
    ret_perm = _partner_matrix(lambda dst: dst + 1 if dst % 2 == 0 else dst - 1)

    w13_1, w2_1 = ffn1_w13.astype(BF16), ffn1_w2.astype(BF16)
    w13_2, w2_2 = ffn2_w13.astype(BF16), ffn2_w2.astype(BF16)
    w_in_p, w_ab, w_abt = _relayout_w_in(w_in)
    w_branch_b, w_out_b = w_branch.astype(BF16), w_out.astype(BF16)
    gain1, gain_mix, gain2 = (g.reshape(DEPTH, 1, D_MODEL) for g in (ffn1_norm, mix_norm, ffn2_norm))
    qk_gain = jnp.concatenate([jnp.tile(attn_q_norm, (1, ATTN_Q_HEADS)),
                               jnp.tile(attn_k_norm, (1, ATTN_KV_HEADS))], axis=1)[:, None, :]
    sinks = jnp.broadcast_to(attn_sinks[:, :, None], (DEPTH, ATTN_Q_HEADS, LANES))
    lane_pad = ((0, 0), (0, LANES - DN_HEADS))
    alog = jnp.pad(dn_a_log, lane_pad)[:, None, :]
    dtb = jnp.pad(dn_dt_bias, lane_pad)[:, None, :]
    row_pad = ((0, 0), (0, DN_HEADS))
    alog_t = jnp.broadcast_to(jnp.pad(dn_a_log, row_pad)[:, :, None], (DEPTH, 2 * DN_HEADS, SEQ_TILE))
    dtb_t = jnp.broadcast_to(jnp.pad(dn_dt_bias, row_pad)[:, :, None], (DEPTH, 2 * DN_HEADS, SEQ_TILE))
    onorm = jnp.tile(dn_out_norm, (1, DN_HEADS))[:, None, :]

    xf = x.reshape(N_TOK, D_MODEL)
    for l in range(DEPTH):
        xf = _ffn_call(xf, mod5, l, 0, gain1, w13_1, w2_1)
        proj, ab, abt = _inproj_call(xf, mod5, l, gain_mix, w_in_p, w_ab, w_abt)
        proj3 = proj.reshape(BATCH, SEQ, N_PROJ)
        out_a = _attn_call(proj3, l, attn_tabs, qk_gain, bd, rope_perm, sinks)
        out_b = _dn_call(proj3, ab, abt, l, dn_conv, alog, dtb, alog_t, dtb_t, onorm, bd)
        out_c = _ret_call(proj3, ret_tabs, ret_perm)
        xf = _merge_call(xf, mod5, l, out_a.reshape(N_TOK, 512), out_b.reshape(N_TOK, 512),
                         out_c.reshape(N_TOK, 512), proj, w_branch_b, w_out_b)
        xf = _ffn_call(xf, mod5, l, 2, gain2, w13_2, w2_2)
    return xf.reshape(BATCH, SEQ, D_MODEL)
```

```python
import math

import numpy as np
import jax
import jax.numpy as jnp
from jax import lax
from jax.experimental import pallas as pl
from jax.experimental.pallas import tpu as pltpu

F32 = jnp.float32
BF16 = jnp.bfloat16
HIGHEST = lax.Precision.HIGHEST

D_MODEL = 1024
BATCH = 8
SEQ = 2048
DEPTH = 2
N_TOK = BATCH * SEQ
N_MOD = 9
D_FF = 2816
EPS = 1e-6
NEG_INF = -1e30

ATTN_Q_HEADS = 8
ATTN_KV_HEADS = 2
ATTN_HEAD_DIM = 64
ATTN_BLOCK = 128
ROPE_DIM = 16
ROPE_THETA = 500000.0
DN_HEADS = 8
DN_DIM = 64
DN_CONV = 4
RET_HEADS = 4
RET_KEY_DIM = 64
RET_THETA = 10000.0

LANES = 128
SUBLANES = 8
MXU_TILE = 256

P_GATES = 0
P_RET = 3072
P_DQKV = 4608
P_Z = 6144
P_ATTN = 6912
N_PROJ = 7680
ATTN_W = 768
ATTN_QK = 640

TM = 512
FFN_CHUNK = 256
N_FFN_CHUNK = D_FF // FFN_CHUNK
PROJ_CHUNK = 768
MOD_TN = 1152
SEQ_TILE = 256
ATTN_TILE = 512
DN_CHUNK = 64
VMEM_LIMIT = 56 * 1024 * 1024


def _dot(a, b, precision=None):
    return jnp.dot(a, b, preferred_element_type=F32, precision=precision)


def _dot_nt(a, b):
    return lax.dot_general(a, b, (((1,), (1,)), ((), ())), preferred_element_type=F32)


def _dot_tn(a, b):
    return lax.dot_general(a, b, (((0,), (0,)), ((), ())), preferred_element_type=F32)


def _split_bf16(a):
    hi = a.astype(BF16)
    return hi, (a - hi.astype(F32)).astype(BF16)


def _split3_bf16(a):
    h1 = a.astype(BF16)
    r1 = a - h1.astype(F32)
    h2 = r1.astype(BF16)
    h3 = (r1 - h2.astype(F32)).astype(BF16)
    return h1, h2, h3


def _dot_exact_rhs(m01, x):
    m = m01.astype(BF16)
    h1, h2, h3 = _split3_bf16(x)
    return _dot(m, h1) + (_dot(m, h2) + _dot(m, h3))


def _dot_exact_lhs(x, m01):
    m = m01.astype(BF16)
    h1, h2, h3 = _split3_bf16(x)
    return _dot(h1, m) + (_dot(h2, m) + _dot(h3, m))


def _sigmoid(x):
    return 0.5 * (jnp.tanh(0.5 * x) + 1.0)


def _silu(x):
    return x * _sigmoid(x)


def _modulate(x, gain, shift, scale):
    ms = jnp.mean(x * x, axis=-1, keepdims=True)
    return x * lax.rsqrt(ms + EPS) * (gain * (1.0 + scale)) + shift


def _lane_block_matmul(pieces, m):
    blk = m.shape[0]
    width = pieces[0].shape[-1]
    outs = []
    for s in range(0, width, blk):
        n = min(blk, width - s)
        mm = m if n == blk else m[:n, :n]
        acc = _dot(pieces[0][:, s:s + n], mm)
        for piece in pieces[1:]:
            acc = acc + _dot(piece[:, s:s + n], mm)
        outs.append(acc)
    return outs[0] if len(outs) == 1 else jnp.concatenate(outs, axis=1)


def _seg_sum(x2, bd):
    return _lane_block_matmul([x2.astype(BF16)], bd)


def _tile_lanes(t, width):
    reps = width // t.shape[-1]
    return t if reps == 1 else jnp.concatenate([t] * reps, axis=1)


def _rotary(x, perm, cos, sin):
    w = x.shape[-1]
    return x * _tile_lanes(cos, w) + _lane_block_matmul(list(_split_bf16(x)), perm) * _tile_lanes(sin, w)


def _cparams(*sem):
    return pltpu.CompilerParams(dimension_semantics=sem, vmem_limit_bytes=VMEM_LIMIT)


def _resident(shape):
    zeros = (0,) * len(shape)
    return pl.BlockSpec(shape, lambda *_: zeros, pipeline_mode=pl.Buffered(1))


def _layer_resident(shape, layer):
    idx = (layer,) + (0,) * len(shape)
    return pl.BlockSpec((1,) + tuple(shape), lambda *_: idx, pipeline_mode=pl.Buffered(1))


def _mod_spec(layer, group, batch_of):
    return pl.BlockSpec((1, 1, 1, 3, D_MODEL), lambda *g: (layer, batch_of(*g), group, 0, 0))


def _mod_kernel(c_ref, w_ref, b_ref, o_ref):
    c = c_ref[...]
    o_ref[0] = _dot(_silu(c), w_ref[0], precision=HIGHEST) + b_ref[0]


def _mod_call(c, w_mod, b_mod):
    n = N_MOD * D_MODEL
    return pl.pallas_call(
        _mod_kernel,
        grid=(DEPTH, n // MOD_TN),
        in_specs=[
            pl.BlockSpec((BATCH, D_MODEL), lambda l, j: (0, 0)),
            pl.BlockSpec((1, D_MODEL, MOD_TN), lambda l, j: (l, 0, j)),
            pl.BlockSpec((1, 1, MOD_TN), lambda l, j: (l, 0, j)),
        ],
        out_specs=pl.BlockSpec((1, BATCH, MOD_TN), lambda l, j: (l, 0, j)),
        out_shape=jax.ShapeDtypeStruct((DEPTH, BATCH, n), F32),
        compiler_params=_cparams("parallel", "parallel"),
        name="adaln_mod",
    )(c, w_mod, b_mod.reshape(DEPTH, 1, n))


def _ffn_half_step(x, mod, gain, w13_ref, w2_ref, h_ref):
    ub = _modulate(x, gain, mod[0:1], mod[1:2]).astype(BF16)
    for j in range(N_FFN_CHUNK):
        lo = j * FFN_CHUNK
        g = _dot(ub, w13_ref[0, :, lo:lo + FFN_CHUNK])
        up = _dot(ub, w13_ref[0, :, D_FF + lo:D_FF + lo + FFN_CHUNK])
        h_ref[:, lo:lo + FFN_CHUNK] = (_silu(g) * up).astype(BF16)
    y = _dot(h_ref[...], w2_ref[0])
    return x + (0.5 * mod[2:3]) * y


def _ffn_kernel(x_ref, mod_ref, gain_ref, w13_ref, w2_ref, o_ref, h_ref):
    o_ref[...] = _ffn_half_step(x_ref[...], mod_ref[0, 0, 0], gain_ref[0], w13_ref, w2_ref, h_ref)


def _ffn_call(x2d, mod5, layer, group, gains, w13, w2):
    tiles_per_batch = SEQ // TM
    return pl.pallas_call(
        _ffn_kernel,
        grid=(N_TOK // TM,),
        in_specs=[
            pl.BlockSpec((TM, D_MODEL), lambda i: (i, 0)),
            _mod_spec(layer, group, lambda i: i // tiles_per_batch),
            _layer_resident((1, D_MODEL), layer),
            _layer_resident((D_MODEL, 2 * D_FF), layer),
            _layer_resident((D_FF, D_MODEL), layer),
        ],
        out_specs=pl.BlockSpec((TM, D_MODEL), lambda i: (i, 0)),
        out_shape=jax.ShapeDtypeStruct((N_TOK, D_MODEL), F32),
        scratch_shapes=[pltpu.VMEM((TM, D_FF), BF16)],
        compiler_params=_cparams("parallel"),
        name="swiglu_ffn",
    )(x2d, mod5, gains, w13, w2)


def _inproj_kernel(x_ref, mod_ref, gain_ref, w_ref, wab_ref, wabt_ref, proj_ref, ab_ref, abt_ref):
    mod = mod_ref[0, 0, 0]
    ub = _modulate(x_ref[...], gain_ref[0], mod[0:1], mod[1:2]).astype(BF16)
    for j in range(N_PROJ // PROJ_CHUNK):
        sl = slice(j * PROJ_CHUNK, (j + 1) * PROJ_CHUNK)
        proj_ref[:, sl] = _dot(ub, w_ref[0, :, sl]).astype(BF16)
    ab_ref[...] = _dot(ub, wab_ref[0])
    abt_ref[0] = _dot_nt(wabt_ref[0], ub)


def _inproj_call(x2d, mod5, layer, gains, w_in_p, w_ab, w_abt):
    tiles_per_batch = SEQ // TM
    return pl.pallas_call(
        _inproj_kernel,
        grid=(N_TOK // TM,),
        in_specs=[
            pl.BlockSpec((TM, D_MODEL), lambda i: (i, 0)),
            _mod_spec(layer, 1, lambda i: i // tiles_per_batch),
            _layer_resident((1, D_MODEL), layer),
            _layer_resident((D_MODEL, N_PROJ), layer),
            _layer_resident((D_MODEL, LANES), layer),
            _layer_resident((2 * DN_HEADS, D_MODEL), layer),
        ],
        out_specs=[
            pl.BlockSpec((TM, N_PROJ), lambda i: (i, 0)),
            pl.BlockSpec((TM, LANES), lambda i: (i, 0)),
            pl.BlockSpec((1, 2 * DN_HEADS, TM), lambda i: (i // tiles_per_batch, 0, i % tiles_per_batch)),
        ],
        out_shape=[
            jax.ShapeDtypeStruct((N_TOK, N_PROJ), BF16),
            jax.ShapeDtypeStruct((N_TOK, LANES), F32),
            jax.ShapeDtypeStruct((BATCH, 2 * DN_HEADS, SEQ), F32),
        ],
        compiler_params=_cparams("parallel"),
        name="mixer_in_proj",
    )(x2d, mod5, gains, w_in_p, w_ab, w_abt)


def _attn_kernel(cur_ref, prev_ref, tc_ref, ts_ref, pc_ref, ps_ref, gain_ref, bd_ref, perm_ref,
                 sink_ref, o_ref):
    t = pl.program_id(1)
    blk = ATTN_BLOCK
    nblk = ATTN_TILE // blk
    per_group = ATTN_Q_HEADS // ATTN_KV_HEADS
    bd = bd_ref[...]
    perm = perm_ref[...]
    gain = gain_ref[0]

    def norm_rope(x, g, cos, sin):
        x = x * lax.rsqrt(_seg_sum(x * x, bd) * (1.0 / ATTN_HEAD_DIM) + EPS) * g
        return _rotary(x, perm, cos, sin)

    cur = cur_ref[0]
    prv = prev_ref[0]
    qk = norm_rope(cur[:, :ATTN_QK].astype(F32), gain, tc_ref[...], ts_ref[...])
    kprev = norm_rope(prv[:, 512:ATTN_QK].astype(F32), gain[:, 512:], pc_ref[...], ps_ref[...])
    q = qk[:, :512] * (ATTN_HEAD_DIM ** -0.5)
    k_all = jnp.concatenate([kprev, qk[:, 512:]], axis=0)
    v_all = jnp.concatenate([prv[:, ATTN_QK:], cur[:, ATTN_QK:]], axis=0).astype(F32)

    low_half = lax.broadcasted_iota(jnp.int32, (1, LANES), 1) < ATTN_HEAD_DIM

    def group_on_both_halves(a, g):
        swapped = pltpu.roll(a, ATTN_HEAD_DIM, 1)
        return (jnp.where(low_half, a, swapped) if g == 0 else jnp.where(low_half, swapped, a)).astype(BF16)

    kd = [group_on_both_halves(k_all, g) for g in range(ATTN_KV_HEADS)]
    vd = [group_on_both_halves(v_all, g) for g in range(ATTN_KV_HEADS)]

    kj = lax.broadcasted_iota(jnp.int32, (2 * blk, blk), 0)
    qi = lax.broadcasted_iota(jnp.int32, (2 * blk, blk), 1)
    in_window = (kj > qi) & (kj <= qi + blk)
    bias = jnp.where(in_window, 0.0, NEG_INF)
    bias_no_prev = jnp.where(in_window & (kj >= blk), 0.0, NEG_INF)
    bias_rest = jnp.concatenate([bias] * per_group, axis=1)
    bias_first = jnp.where(t == 0, jnp.concatenate([bias_no_prev] * per_group, axis=1), bias_rest)
    sink_row = lax.broadcasted_iota(jnp.int32, (2 * blk, per_group * blk), 0) == 0
    sinks = sink_ref[0]
    ones_blk = jnp.ones((2 * blk, LANES), BF16)
    first_key = lax.broadcasted_iota(jnp.int32, (2 * blk, LANES), 0) == 0

    units = [(i, g) for i in range(nblk) for g in range(ATTN_KV_HEADS)]
    logits = []
    for i, g in units:
        rows = slice(i * blk, (i + 1) * blk)
        parts = []
        for p in range(g * per_group // 2, (g + 1) * per_group // 2):
            qp = q[rows, p * LANES:(p + 1) * LANES]
            parts += [jnp.where(low_half, qp, 0.0), jnp.where(low_half, 0.0, qp)]
        qs = jnp.concatenate(parts, axis=0).astype(BF16)
        band = kd[g][i * blk:(i + 2) * blk]
        sink = jnp.concatenate(
            [jnp.broadcast_to(sinks[g * per_group + r:g * per_group + r + 1, :], (2 * blk, LANES))
             for r in range(per_group)], axis=1)
        lg = _dot_nt(band, qs) + (bias_first if i == 0 else bias_rest)
        logits.append(jnp.where(sink_row, sink, lg))
    expd = []
    for lg in logits:
        m = jnp.max(lg, axis=0, keepdims=True)
        expd.append(jnp.exp(lg - m).astype(BF16))
    for (i, g), e in zip(units, expd):
        vals = jnp.where(first_key, 0.0, vd[g][i * blk:(i + 2) * blk].astype(F32)).astype(BF16)
        o = _dot_tn(e, jnp.concatenate([vals, ones_blk], axis=1))
        o = o[:, :LANES] * (1.0 / o[:, LANES:])
        for j in range(per_group // 2):
            p = g * per_group // 2 + j
            pair = jnp.where(low_half, o[2 * j * blk:(2 * j + 1) * blk], o[(2 * j + 1) * blk:(2 * j + 2) * blk])
            o_ref[0, i * blk:(i + 1) * blk, p * LANES:(p + 1) * LANES] = pair.astype(BF16)


def _attn_call(proj3, layer, tabs, gains, bd, perm, sinks):
    nblk = ATTN_TILE // ATTN_BLOCK
    col = P_ATTN // ATTN_W
    tab_cur = pl.BlockSpec((ATTN_TILE, LANES), lambda b, t: (t, 0))
    tab_prev = pl.BlockSpec((ATTN_BLOCK, LANES), lambda b, t: (jnp.maximum(t * nblk - 1, 0), 0))
    return pl.pallas_call(
        _attn_kernel,
        grid=(BATCH, SEQ // ATTN_TILE),
        in_specs=[
            pl.BlockSpec((1, ATTN_TILE, ATTN_W), lambda b, t: (b, t, col)),
            pl.BlockSpec((1, ATTN_BLOCK, ATTN_W), lambda b, t: (b, jnp.maximum(t * nblk - 1, 0), col)),
            tab_cur, tab_cur, tab_prev, tab_prev,
            _layer_resident((1, ATTN_QK), layer),
            _resident((MXU_TILE, MXU_TILE)),
            _resident((MXU_TILE, MXU_TILE)),
            _layer_resident((ATTN_Q_HEADS, LANES), layer),
        ],
        out_specs=pl.BlockSpec((1, ATTN_TILE, 512), lambda b, t: (b, t, 0)),
        out_shape=jax.ShapeDtypeStruct((BATCH, SEQ, 512), BF16),
        compiler_params=_cparams("parallel", "parallel"),
        name="swa_attention",
    )(proj3, proj3, tabs[0], tabs[1], tabs[0], tabs[1], gains, bd, perm, sinks)


def _softplus(x):
    return jnp.maximum(x, 0.0) + jnp.log(1.0 + jnp.exp(-jnp.abs(x)))


def _dn_kernel(cur_ref, prev_ref, z_ref, ab_ref, abt_ref, conv_ref, alog_ref, dtb_ref,
               alogt_ref, dtbt_ref, onorm_ref, bd_ref, o_ref, state_ref, y_ref, c_ref, r_ref):
    t = pl.program_id(1)
    T = SEQ_TILE
    C = DN_CHUNK
    nh = DN_HEADS
    width = nh * DN_DIM

    @pl.when(t == 0)
    def _():
        state_ref[...] = jnp.zeros_like(state_ref)

    x = cur_ref[0].astype(F32)
    tail = jnp.where(t > 0, prev_ref[0].astype(F32)[SUBLANES:], 0.0)
    xc = jnp.concatenate([tail, x], axis=0)
    w = conv_ref[0]
    y = xc[SUBLANES:SUBLANES + T] * w[3:4]
    for j in range(DN_CONV - 1):
        off = SUBLANES - (DN_CONV - 1) + j
        y = y + xc[off:off + T] * w[j:j + 1]
    y = _silu(y)
    bd = bd_ref[...]
    q = y[:, :width]
    k = y[:, width:2 * width]
    v = y[:, 2 * width:]
    q = q * lax.rsqrt(_seg_sum(q * q, bd) + EPS) * (DN_DIM ** -0.5)
    k = k * lax.rsqrt(_seg_sum(k * k, bd) + EPS)

    ab = ab_ref[...]
    la = -jnp.exp(alog_ref[0]) * _softplus(ab + dtb_ref[0])
    beta = _sigmoid(ab)
    lat = -jnp.exp(alogt_ref[0]) * _softplus(abt_ref[0] + dtbt_ref[0])
    ri = lax.broadcasted_iota(jnp.int32, (T, T), 0)
    ci = lax.broadcasted_iota(jnp.int32, (T, T), 1)
    same = (ri // C) == (ci // C)
    causal = same & (ci <= ri)
    strict = same & (ci < ri)
    eye = (ri == ci).astype(F32)
    g_nat = _dot_exact_rhs(causal, la)
    g_t = _dot_exact_lhs(lat, same & (ri <= ci))
    gend_nat = jnp.concatenate(
        [jnp.broadcast_to(g_nat[c * C + C - 1:c * C + C, :], (C, LANES)) for c in range(T // C)], axis=0)

    low_half = lax.broadcasted_iota(jnp.int32, (1, LANES), 1) < DN_DIM
    high_half = jnp.logical_not(low_half)
    bdr = lax.broadcasted_iota(jnp.int32, (LANES, LANES), 0) // DN_DIM
    bdc = lax.broadcasted_iota(jnp.int32, (LANES, LANES), 1) // DN_DIM
    blockdiag = bdr == bdc

    def per_head_lanes(src, first):
        return jnp.where(low_half, src[:, first:first + 1], src[:, first + 1:first + 2])

    npair = nh // 2
    halves = (low_half, high_half)
    qd, kt, dec_end, rhs, pw, xinv, intras = [], [], [], [], [], [], []
    for p in range(npair):
        sl = slice(p * LANES, (p + 1) * LANES)
        gx = per_head_lanes(g_nat, 2 * p)
        gendx = per_head_lanes(gend_nat, 2 * p)
        bx = per_head_lanes(beta, nh + 2 * p)
        eg = jnp.exp(gx)
        k_p = k[:, sl]
        q_p = q[:, sl]
        kb = k_p * bx
        vb = v[:, sl] * bx
        kbe = kb * eg
        qd.append((q_p * eg).astype(BF16))
        kt.append((k_p * jnp.exp(gendx - gx)).astype(BF16))
        dec_end.append(jnp.exp(gendx))
        k_pb = k_p.astype(BF16)
        for half in range(2):
            h = 2 * p + half
            mh = halves[half]
            diff = g_nat[:, h:h + 1] - g_t[h:h + 1, :]
            decay = jnp.exp(jnp.where(causal, diff, NEG_INF))
            kk = _dot_nt(jnp.where(mh, kb, 0.0).astype(BF16), k_pb)
            qkm = _dot_nt(jnp.where(mh, q_p, 0.0).astype(BF16), k_pb)
            intras.append((qkm * decay).astype(BF16))
            n1 = jnp.where(strict, -(kk * decay), 0.0).astype(BF16)
            pw.append(n1)
            xinv.append(eye + n1.astype(F32))
            rhs.append(jnp.concatenate([jnp.where(mh, vb, 0.0), jnp.where(mh, kbe, 0.0)], axis=1))
    n1s = list(pw)
    for _ in range(5):
        for h in range(nh):
            pw[h] = _dot(pw[h], pw[h]).astype(BF16)
            xinv[h] = xinv[h] + _dot(xinv[h].astype(BF16), pw[h])
    xb = [xinv[h].astype(BF16) for h in range(nh)]
    one = bd[0:1, 0:1].astype(F32)
    for h in range(nh):
        y_ref[h] = _dot(xb[h], rhs[h].astype(BF16))
    for h in range(nh):
        yh, yl = _split_bf16(y_ref[h])
        c_ref[h] = _dot(n1s[h], yh) + _dot(n1s[h], yl)
    for h in range(nh):
        r_ref[h] = ((rhs[h] - y_ref[h]) + c_ref[h]).astype(BF16)
    for h in range(nh):
        c_ref[h] = _dot(xb[h], r_ref[h])
    for h in range(nh):
        y_ref[h] = y_ref[h] + one * c_ref[h]
    u, w_ = [], []
    for p in range(npair):
        u.append(y_ref[2 * p, :, :LANES] + y_ref[2 * p + 1, :, :LANES])
        w_.append((y_ref[2 * p, :, LANES:] + y_ref[2 * p + 1, :, LANES:]).astype(BF16))
    state = [state_ref[p] for p in range(npair)]
    vn = [[] for _ in range(npair)]
    oi = [[] for _ in range(npair)]
    for c in range(T // C):
        rows = slice(c * C, (c + 1) * C)
        for p in range(npair):
            sb = state[p].astype(BF16)
            vn_c = u[p][rows] - _dot(w_[p][rows], sb)
            oi[p].append(_dot(qd[p][rows], sb))
            upd = _dot_tn(kt[p][rows], vn_c.astype(BF16))
            state[p] = state[p] * dec_end[p][c * C:c * C + 1] + jnp.where(blockdiag, upd, 0.0)
            vn[p].append(vn_c)
    outs = []
    for p in range(npair):
        state_ref[p] = state[p]
        vn_all = jnp.concatenate(vn[p], axis=0)
        o_p = jnp.concatenate(oi[p], axis=0)
        for half in range(2):
            o_p = o_p + _dot(intras[2 * p + half], jnp.where(halves[half], vn_all, 0.0).astype(BF16))
        outs.append(o_p)
    o = jnp.concatenate(outs, axis=1)
    ms = _seg_sum(o * o, bd) * (1.0 / DN_DIM)
    zg = _silu(z_ref[0].astype(F32))
    o_ref[0] = (o * lax.rsqrt(ms + EPS) * onorm_ref[0] * zg).astype(BF16)


def _dn_call(proj3, ab, abt, layer, conv_w, alog, dtb, alog_t, dtb_t, onorm, bd):
    T = SEQ_TILE
    tail_rows = 2 * SUBLANES
    per_tile = T // tail_rows
    return pl.pallas_call(
        _dn_kernel,
        grid=(BATCH, SEQ // T),
        in_specs=[
            pl.BlockSpec((1, T, 1536), lambda b, t: (b, t, P_DQKV // 1536)),
            pl.BlockSpec((1, tail_rows, 1536), lambda b, t: (b, jnp.maximum(t * per_tile - 1, 0), P_DQKV // 1536)),
            pl.BlockSpec((1, T, 512), lambda b, t: (b, t, P_Z // 512)),
            pl.BlockSpec((T, LANES), lambda b, t: (b * (SEQ // T) + t, 0)),
            pl.BlockSpec((1, 2 * DN_HEADS, T), lambda b, t: (b, 0, t)),
            _layer_resident((DN_CONV, 1536), layer),
            _layer_resident((1, LANES), layer),
            _layer_resident((1, LANES), layer),
            _layer_resident((2 * DN_HEADS, T), layer),
            _layer_resident((2 * DN_HEADS, T), layer),
            _layer_resident((1, 512), layer),
            _resident((MXU_TILE, MXU_TILE)),
        ],
        out_specs=pl.BlockSpec((1, T, 512), lambda b, t: (b, t, 0)),
        out_shape=jax.ShapeDtypeStruct((BATCH, SEQ, 512), BF16),
        scratch_shapes=[pltpu.VMEM((DN_HEADS // 2, LANES, LANES), F32),
                        pltpu.VMEM((DN_HEADS, T, 2 * LANES), F32),
                        pltpu.VMEM((DN_HEADS, T, 2 * LANES), F32),
                        pltpu.VMEM((DN_HEADS, T, 2 * LANES), BF16)],
        compiler_params=_cparams("parallel", "arbitrary"),
        name="gated_deltanet",
    )(proj3, proj3, proj3, ab, abt, conv_w, alog, dtb, alog_t, dtb_t, onorm, bd)


def _ret_kernel(x_ref, tc_ref, ts_ref, perm_ref, o_ref, state_ref):
    t = pl.program_id(1)
    T = SEQ_TILE

    @pl.when(t == 0)
    def _():
        state_ref[...] = jnp.zeros_like(state_ref)

    x = x_ref[0]
    qk = _rotary(x[:, :512].astype(F32), perm_ref[...], tc_ref[...], ts_ref[...])
    rq = qk[:, :256]
    rk = qk[:, 256:] * (RET_KEY_DIM ** -0.5)
    ri = lax.broadcasted_iota(jnp.int32, (T, T), 0)
    ci = lax.broadcasted_iota(jnp.int32, (T, T), 1)
    causal = ci <= ri
    dist = (ri - ci).astype(F32)
    row = lax.broadcasted_iota(jnp.int32, (T, LANES), 0).astype(F32)
    low_half = lax.broadcasted_iota(jnp.int32, (1, LANES), 1) < RET_KEY_DIM

    for h in range(RET_HEADS):
        log_gamma = math.log1p(-(2.0 ** (-5.0 - h)))
        p = h // 2
        mh = low_half if h % 2 == 0 else jnp.logical_not(low_half)
        q_p = rq[:, p * LANES:(p + 1) * LANES]
        k_p = rk[:, p * LANES:(p + 1) * LANES]
        v_h = x[:, 512 + h * LANES:512 + (h + 1) * LANES]
        decay = jnp.exp(jnp.where(causal, dist * log_gamma, NEG_INF))
        scores = _dot_nt(jnp.where(mh, q_p, 0.0).astype(BF16), k_p.astype(BF16)) * decay
        state = state_ref[h]
        q_dec = jnp.where(mh, q_p * jnp.exp(log_gamma * (row + 1.0)), 0.0)
        out = _dot(scores.astype(BF16), v_h) + _dot(q_dec.astype(BF16), state.astype(BF16))
        k_end = jnp.where(mh, k_p * jnp.exp(log_gamma * (T - 1.0 - row)), 0.0)
        state_ref[h] = state * math.exp(log_gamma * T) + _dot_tn(k_end.astype(BF16), v_h)
        ms = jnp.mean(out * out, axis=-1, keepdims=True)
        g = _silu(x[:, 1024 + h * LANES:1024 + (h + 1) * LANES].astype(F32))
        o_ref[0, :, h * LANES:(h + 1) * LANES] = (out * lax.rsqrt(ms + EPS) * g).astype(BF16)


def _ret_call(proj3, tabs, perm):
    T = SEQ_TILE
    tab = pl.BlockSpec((T, LANES), lambda b, t: (t, 0))
    return pl.pallas_call(
        _ret_kernel,
        grid=(BATCH, SEQ // T),
        in_specs=[pl.BlockSpec((1, T, 1536), lambda b, t: (b, t, P_RET // 1536)), tab, tab,
                  _resident((MXU_TILE, MXU_TILE))],
        out_specs=pl.BlockSpec((1, T, 512), lambda b, t: (b, t, 0)),
        out_shape=jax.ShapeDtypeStruct((BATCH, SEQ, 512), BF16),
        scratch_shapes=[pltpu.VMEM((RET_HEADS, LANES, LANES), F32)],
        compiler_params=_cparams("parallel", "arbitrary"),
        name="retention",
    )(proj3, tabs[0], tabs[1], perm)


def _merge_ffn_kernel(x_ref, mod_mix_ref, mod_ffn_ref, a_ref, b_ref, c_ref, g_ref, wb_ref, wo_ref,
                      gain_ref, w13_ref, w2_ref, o_ref, h_ref):
    merged = jnp.zeros((TM, D_MODEL), F32)
    for i, br in enumerate((a_ref, b_ref, c_ref)):
        gate = _sigmoid(g_ref[:, i * D_MODEL:(i + 1) * D_MODEL].astype(F32))
        merged = merged + gate * _dot(br[...], wb_ref[0, i])
    y = _dot(merged.astype(BF16), wo_ref[0])
    x = x_ref[...] + mod_mix_ref[0, 0, 0][2:3] * y
    o_ref[...] = _ffn_half_step(x, mod_ffn_ref[0, 0, 0], gain_ref[0], w13_ref, w2_ref, h_ref)


def _merge_ffn_call(x2d, mod5, layer, out_a, out_b, out_c, proj2, w_branch, w_out, gains, w13, w2):
    tiles_per_batch = SEQ // TM
    batch_of = lambda i: i // tiles_per_batch
    branch = pl.BlockSpec((TM, 512), lambda i: (i, 0))
    return pl.pallas_call(
        _merge_ffn_kernel,
        grid=(N_TOK // TM,),
        in_specs=[
            pl.BlockSpec((TM, D_MODEL), lambda i: (i, 0)),
            _mod_spec(layer, 1, batch_of),
            _mod_spec(layer, 2, batch_of),
            branch, branch, branch,
            pl.BlockSpec((TM, 3 * D_MODEL), lambda i: (i, P_GATES // (3 * D_MODEL))),
            _layer_resident((3, 512, D_MODEL), layer),
            _layer_resident((D_MODEL, D_MODEL), layer),
            _layer_resident((1, D_MODEL), layer),
            _layer_resident((D_MODEL, 2 * D_FF), layer),
            _layer_resident((D_FF, D_MODEL), layer),
        ],
        out_specs=pl.BlockSpec((TM, D_MODEL), lambda i: (i, 0)),
        out_shape=jax.ShapeDtypeStruct((N_TOK, D_MODEL), F32),
        scratch_shapes=[pltpu.VMEM((TM, D_FF), BF16)],
        compiler_params=_cparams("parallel"),
        name="merge_ffn",
    )(x2d, mod5, mod5, out_a, out_b, out_c, proj2, w_branch, w_out, gains, w13, w2)


def _block_diag_ones(n, seg):
    idx = np.arange(n) // seg
    return jnp.asarray((idx[:, None] == idx[None, :]).astype(np.float32), dtype=BF16)


def _partner_matrix(partner_of):
    m = np.zeros((MXU_TILE, MXU_TILE), np.float32)
    for dst in range(MXU_TILE):
        src = partner_of(dst)
        if src is not None:
            m[src, dst] = 1.0
    return jnp.asarray(m, dtype=BF16)


def _rope_partner(dst):
    d = dst % ATTN_HEAD_DIM
    half = ROPE_DIM // 2
    if d < half:
        return dst + half
    return dst - half if d < ROPE_DIM else None


def _attn_tables():
    half = ROPE_DIM // 2
    pos = jnp.arange(SEQ, dtype=F32)
    inv_freq = ROPE_THETA ** (-jnp.arange(0, ROPE_DIM, 2, dtype=F32) / ROPE_DIM)
    phase = pos[:, None] * inv_freq[None, :]
    cos, sin = jnp.cos(phase), jnp.sin(phase)
    pad = ATTN_HEAD_DIM - ROPE_DIM
    c = jnp.concatenate([cos, cos, jnp.ones((SEQ, pad), F32)], axis=1)
    s = jnp.concatenate([-sin, sin, jnp.zeros((SEQ, pad), F32)], axis=1)
    return jnp.tile(c, (1, 2)), jnp.tile(s, (1, 2))


def _ret_tables():
    pos = jnp.arange(SEQ, dtype=F32)
    angle = 1.0 / (RET_THETA ** jnp.linspace(0.0, 1.0, RET_KEY_DIM // 2, dtype=F32))
    angle = jnp.repeat(angle, 2)
    phase = pos[:, None] * angle[None, :]
    cos, sin = jnp.cos(phase), jnp.sin(phase)
    even = (jnp.arange(RET_KEY_DIM) % 2 == 0)[None, :]
    return jnp.tile(cos, (1, 2)), jnp.tile(jnp.where(even, -sin, sin), (1, 2))


W_IN_ROWS = 128
N_IN = 7440


def _w_in_relayout_kernel(w_ref, o_ref):
    w = w_ref[0]
    cs = lambda a, b: w[:, a:b]
    pad = jnp.zeros((W_IN_ROWS, P_ATTN - P_Z - 512), F32)
    cols = [cs(4368, 7440), cs(2832, 4368), cs(768, 2304), cs(2320, 2832), pad, cs(0, 768)]
    o_ref[0] = jnp.concatenate(cols, axis=1).astype(BF16)


def _relayout_w_in(w):
    w_p = pl.pallas_call(
        _w_in_relayout_kernel,
        grid=(DEPTH, D_MODEL // W_IN_ROWS),
        in_specs=[pl.BlockSpec((1, W_IN_ROWS, N_IN), lambda l, i: (l, i, 0))],
        out_specs=pl.BlockSpec((1, W_IN_ROWS, N_PROJ), lambda l, i: (l, i, 0)),
        out_shape=jax.ShapeDtypeStruct((DEPTH, D_MODEL, N_PROJ), BF16),
        compiler_params=_cparams("parallel", "parallel"),
        name="w_in_relayout",
    )(w)
    ab = w[:, :, 2304:2320]
    w_ab = jnp.concatenate([ab, jnp.zeros((DEPTH, D_MODEL, LANES - 2 * DN_HEADS), F32)], axis=2).astype(BF16)
    return w_p, w_ab, jnp.transpose(ab, (0, 2, 1)).astype(BF16)


def kernel(x, c, w_mod, b_mod, ffn1_norm, ffn1_w13, ffn1_w2, mix_norm, w_in, attn_q_norm, attn_k_norm,
           attn_sinks, dn_conv, dn_a_log, dn_dt_bias, dn_out_norm, w_branch, w_out, ffn2_norm, ffn2_w13,
           ffn2_w2):
    mod5 = _mod_call(c, w_mod, b_mod).reshape(DEPTH, BATCH, 3, 3, D_MODEL)
    attn_tabs = _attn_tables()
    ret_tabs = _ret_tables()
    bd = _block_diag_ones(MXU_TILE, ATTN_HEAD_DIM)
    rope_perm = _partner_matrix(_rope_partner)
    ret_perm = _partner_matrix(lambda dst: dst + 1 if dst % 2 == 0 else dst - 1)

    w13_1, w2_1 = ffn1_w13.astype(BF16), ffn1_w2.astype(BF16)
    w13_2, w2_2 = ffn2_w13.astype(BF16), ffn2_w2.astype(BF16)
    w_in_p, w_ab, w_abt = _relayout_w_in(w_in)
    w_branch_b, w_out_b = w_branch.astype(BF16), w_out.astype(BF16)
    gain1, gain_mix, gain2 = (g.reshape(DEPTH, 1, D_MODEL) for g in (ffn1_norm, mix_norm, ffn2_norm))
    qk_gain = jnp.concatenate([jnp.tile(attn_q_norm, (1, ATTN_Q_HEADS)),
                               jnp.tile(attn_k_norm, (1, ATTN_KV_HEADS))], axis=1)[:, None, :]
    sinks = jnp.broadcast_to(attn_sinks[:, :, None], (DEPTH, ATTN_Q_HEADS, LANES))
    lane_pad = ((0, 0), (0, LANES - DN_HEADS))
    alog = jnp.pad(dn_a_log, lane_pad)[:, None, :]
    dtb = jnp.pad(dn_dt_bias, lane_pad)[:, None, :]
    row_pad = ((0, 0), (0, DN_HEADS))
    alog_t = jnp.broadcast_to(jnp.pad(dn_a_log, row_pad)[:, :, None], (DEPTH, 2 * DN_HEADS, SEQ_TILE))
    dtb_t = jnp.broadcast_to(jnp.pad(dn_dt_bias, row_pad)[:, :, None], (DEPTH, 2 * DN_HEADS, SEQ_TILE))
    onorm = jnp.tile(dn_out_norm, (1, DN_HEADS))[:, None, :]

    xf = x.reshape(N_TOK, D_MODEL)
    for l in range(DEPTH):
        xf = _ffn_call(xf, mod5, l, 0, gain1, w13_1, w2_1)
        proj, ab, abt = _inproj_call(xf, mod5, l, gain_mix, w_in_p, w_ab, w_abt)
        proj3 = proj.reshape(BATCH, SEQ, N_PROJ)
        out_a = _attn_call(proj3, l, attn_tabs, qk_gain, bd, rope_perm, sinks)
        out_b = _dn_call(proj3, ab, abt, l, dn_conv, alog, dtb, alog_t, dtb_t, onorm, bd)
        out_c = _ret_call(proj3, ret_tabs, ret_perm)
        xf = _merge_ffn_call(xf, mod5, l, out_a.reshape(N_TOK, 512), out_b.reshape(N_TOK, 512),
                             out_c.reshape(N_TOK, 512), proj, w_branch_b, w_out_b, gain2, w13_2, w2_2)
    return xf.reshape(BATCH, SEQ, D_MODEL)
```

```python
import math

import numpy as np
import jax
import jax.numpy as jnp
from jax import lax
from jax.experimental import pallas as pl
from jax.experimental.pallas import tpu as pltpu

F32 = jnp.float32
BF16 = jnp.bfloat16
HIGHEST = lax.Precision.HIGHEST

D_MODEL = 1024
BATCH = 8
SEQ = 2048
DEPTH = 2
N_TOK = BATCH * SEQ
N_MOD = 9
D_FF = 2816
EPS = 1e-6
NEG_INF = -1e30

ATTN_Q_HEADS = 8
ATTN_KV_HEADS = 2
ATTN_HEAD_DIM = 64
ATTN_BLOCK = 128
ROPE_DIM = 16
ROPE_THETA = 500000.0
DN_HEADS = 8
DN_DIM = 64
DN_CONV = 4
RET_HEADS = 4
RET_KEY_DIM = 64
RET_THETA = 10000.0

LANES = 128
SUBLANES = 8
MXU_TILE = 256

P_GATES = 0
P_RET = 3072
P_DQKV = 4608
P_Z = 6144
P_ATTN = 6912
N_PROJ = 7680
ATTN_W = 768
ATTN_QK = 640

TM = 512
FFN_CHUNK = 256
N_FFN_CHUNK = D_FF // FFN_CHUNK
PROJ_CHUNK = 768
MOD_TN = 1152
SEQ_TILE = 128
RET_TILE = 256
ATTN_TILE = 512
DN_CHUNK = 64
VMEM_LIMIT = 56 * 1024 * 1024


def _dot(a, b, precision=None):
    return jnp.dot(a, b, preferred_element_type=F32, precision=precision)


def _dot_nt(a, b):
    return lax.dot_general(a, b, (((1,), (1,)), ((), ())), preferred_element_type=F32)


def _dot_tn(a, b):
    return lax.dot_general(a, b, (((0,), (0,)), ((), ())), preferred_element_type=F32)


def _split_bf16(a):
    hi = a.astype(BF16)
    return hi, (a - hi.astype(F32)).astype(BF16)


def _split3_bf16(a):
    h1 = a.astype(BF16)
    r1 = a - h1.astype(F32)
    h2 = r1.astype(BF16)
    h3 = (r1 - h2.astype(F32)).astype(BF16)
    return h1, h2, h3


def _dot_exact_rhs(m01, x):
    m = m01.astype(BF16)
    h1, h2, h3 = _split3_bf16(x)
    return _dot(m, h1) + (_dot(m, h2) + _dot(m, h3))


def _dot_exact_lhs(x, m01):
    m = m01.astype(BF16)
    h1, h2, h3 = _split3_bf16(x)
    return _dot(h1, m) + (_dot(h2, m) + _dot(h3, m))


def _sigmoid(x):
    return 0.5 * (jnp.tanh(0.5 * x) + 1.0)


def _silu(x):
    return x * _sigmoid(x)


def _modulate(x, gain, shift, scale):
    ms = jnp.mean(x * x, axis=-1, keepdims=True)
    return x * lax.rsqrt(ms + EPS) * (gain * (1.0 + scale)) + shift


def _lane_block_matmul(pieces, m):
    blk = m.shape[0]
    width = pieces[0].shape[-1]
    outs = []
    for s in range(0, width, blk):
        n = min(blk, width - s)
        mm = m if n == blk else m[:n, :n]
        acc = _dot(pieces[0][:, s:s + n], mm)
        for piece in pieces[1:]:
            acc = acc + _dot(piece[:, s:s + n], mm)
        outs.append(acc)
    return outs[0] if len(outs) == 1 else jnp.concatenate(outs, axis=1)


def _seg_sum(x2, bd):
    return _lane_block_matmul([x2.astype(BF16)], bd)


def _tile_lanes(t, width):
    reps = width // t.shape[-1]
    return t if reps == 1 else jnp.concatenate([t] * reps, axis=1)


def _rotary(x, perm, cos, sin):
    w = x.shape[-1]
    return x * _tile_lanes(cos, w) + _lane_block_matmul(list(_split_bf16(x)), perm) * _tile_lanes(sin, w)


def _cparams(*sem):
    return pltpu.CompilerParams(dimension_semantics=sem, vmem_limit_bytes=VMEM_LIMIT)


def _resident(shape):
    zeros = (0,) * len(shape)
    return pl.BlockSpec(shape, lambda *_: zeros, pipeline_mode=pl.Buffered(1))


def _layer_resident(shape, layer):
    idx = (layer,) + (0,) * len(shape)
    return pl.BlockSpec((1,) + tuple(shape), lambda *_: idx, pipeline_mode=pl.Buffered(1))


def _mod_spec(layer, group, batch_of):
    return pl.BlockSpec((1, 1, 1, 3, D_MODEL), lambda *g: (layer, batch_of(*g), group, 0, 0))


def _mod_kernel(c_ref, w_ref, b_ref, o_ref):
    c = c_ref[...]
    o_ref[0] = _dot(_silu(c), w_ref[0], precision=HIGHEST) + b_ref[0]


def _mod_call(c, w_mod, b_mod):
    n = N_MOD * D_MODEL
    return pl.pallas_call(
        _mod_kernel,
        grid=(DEPTH, n // MOD_TN),
        in_specs=[
            pl.BlockSpec((BATCH, D_MODEL), lambda l, j: (0, 0)),
            pl.BlockSpec((1, D_MODEL, MOD_TN), lambda l, j: (l, 0, j)),
            pl.BlockSpec((1, 1, MOD_TN), lambda l, j: (l, 0, j)),
        ],
        out_specs=pl.BlockSpec((1, BATCH, MOD_TN), lambda l, j: (l, 0, j)),
        out_shape=jax.ShapeDtypeStruct((DEPTH, BATCH, n), F32),
        compiler_params=_cparams("parallel", "parallel"),
        name="adaln_mod",
    )(c, w_mod, b_mod.reshape(DEPTH, 1, n))


def _ffn_half_step(x, mod, gain, w13_ref, w2_ref, h_ref):
    ub = _modulate(x, gain, mod[0:1], mod[1:2]).astype(BF16)
    for j in range(N_FFN_CHUNK):
        lo = j * FFN_CHUNK
        g = _dot(ub, w13_ref[0, :, lo:lo + FFN_CHUNK])
        up = _dot(ub, w13_ref[0, :, D_FF + lo:D_FF + lo + FFN_CHUNK])
        h_ref[:, lo:lo + FFN_CHUNK] = (_silu(g) * up).astype(BF16)
    y = _dot(h_ref[...], w2_ref[0])
    return x + (0.5 * mod[2:3]) * y


def _ffn_kernel(x_ref, mod_ref, gain_ref, w13_ref, w2_ref, o_ref, h_ref):
    o_ref[...] = _ffn_half_step(x_ref[...], mod_ref[0, 0, 0], gain_ref[0], w13_ref, w2_ref, h_ref)


def _ffn_call(x2d, mod5, layer, group, gains, w13, w2):
    tiles_per_batch = SEQ // TM
    return pl.pallas_call(
        _ffn_kernel,
        grid=(N_TOK // TM,),
        in_specs=[
            pl.BlockSpec((TM, D_MODEL), lambda i: (i, 0)),
            _mod_spec(layer, group, lambda i: i // tiles_per_batch),
            _layer_resident((1, D_MODEL), layer),
            _layer_resident((D_MODEL, 2 * D_FF), layer),
            _layer_resident((D_FF, D_MODEL), layer),
        ],
        out_specs=pl.BlockSpec((TM, D_MODEL), lambda i: (i, 0)),
        out_shape=jax.ShapeDtypeStruct((N_TOK, D_MODEL), F32),
        scratch_shapes=[pltpu.VMEM((TM, D_FF), BF16)],
        compiler_params=_cparams("parallel"),
        name="swiglu_ffn",
    )(x2d, mod5, gains, w13, w2)


def _inproj_kernel(x_ref, mod_ref, gain_ref, w_ref, wab_ref, wabt_ref, proj_ref, ab_ref, abt_ref):
    mod = mod_ref[0, 0, 0]
    ub = _modulate(x_ref[...], gain_ref[0], mod[0:1], mod[1:2]).astype(BF16)
    for j in range(N_PROJ // PROJ_CHUNK):
        sl = slice(j * PROJ_CHUNK, (j + 1) * PROJ_CHUNK)
        proj_ref[:, sl] = _dot(ub, w_ref[0, :, sl]).astype(BF16)
    ab_ref[...] = _dot(ub, wab_ref[0])
    abt_ref[0] = _dot_nt(wabt_ref[0], ub)


def _inproj_call(x2d, mod5, layer, gains, w_in_p, w_ab, w_abt):
    tiles_per_batch = SEQ // TM
    return pl.pallas_call(
        _inproj_kernel,
        grid=(N_TOK // TM,),
        in_specs=[
            pl.BlockSpec((TM, D_MODEL), lambda i: (i, 0)),
            _mod_spec(layer, 1, lambda i: i // tiles_per_batch),
            _layer_resident((1, D_MODEL), layer),
            _layer_resident((D_MODEL, N_PROJ), layer),
            _layer_resident((D_MODEL, LANES), layer),
            _layer_resident((2 * DN_HEADS, D_MODEL), layer),
        ],
        out_specs=[
            pl.BlockSpec((TM, N_PROJ), lambda i: (i, 0)),
            pl.BlockSpec((TM, LANES), lambda i: (i, 0)),
            pl.BlockSpec((1, 2 * DN_HEADS, TM), lambda i: (i // tiles_per_batch, 0, i % tiles_per_batch)),
        ],
        out_shape=[
            jax.ShapeDtypeStruct((N_TOK, N_PROJ), BF16),
            jax.ShapeDtypeStruct((N_TOK, LANES), F32),
            jax.ShapeDtypeStruct((BATCH, 2 * DN_HEADS, SEQ), F32),
        ],
        compiler_params=_cparams("parallel"),
        name="mixer_in_proj",
    )(x2d, mod5, gains, w_in_p, w_ab, w_abt)


def _attn_kernel(cur_ref, prev_ref, tc_ref, ts_ref, pc_ref, ps_ref, gain_ref, bd_ref, perm_ref,
                 sink_ref, o_ref):
    t = pl.program_id(1)
    blk = ATTN_BLOCK
    nblk = ATTN_TILE // blk
    per_group = ATTN_Q_HEADS // ATTN_KV_HEADS
    bd = bd_ref[...]
    perm = perm_ref[...]
    gain = gain_ref[0]

    def norm_rope(x, g, cos, sin):
        x = x * lax.rsqrt(_seg_sum(x * x, bd) * (1.0 / ATTN_HEAD_DIM) + EPS) * g
        return _rotary(x, perm, cos, sin)

    cur = cur_ref[0]
    prv = prev_ref[0]
    qk = norm_rope(cur[:, :ATTN_QK].astype(F32), gain, tc_ref[...], ts_ref[...])
    kprev = norm_rope(prv[:, 512:ATTN_QK].astype(F32), gain[:, 512:], pc_ref[...], ps_ref[...])
    q = qk[:, :512] * (ATTN_HEAD_DIM ** -0.5)
    k_all = jnp.concatenate([kprev, qk[:, 512:]], axis=0)
    v_all = jnp.concatenate([prv[:, ATTN_QK:], cur[:, ATTN_QK:]], axis=0).astype(F32)

    low_half = lax.broadcasted_iota(jnp.int32, (1, LANES), 1) < ATTN_HEAD_DIM

    def group_on_both_halves(a, g):
        swapped = pltpu.roll(a, ATTN_HEAD_DIM, 1)
        return (jnp.where(low_half, a, swapped) if g == 0 else jnp.where(low_half, swapped, a)).astype(BF16)

    kd = [group_on_both_halves(k_all, g) for g in range(ATTN_KV_HEADS)]
    vd = [group_on_both_halves(v_all, g) for g in range(ATTN_KV_HEADS)]

    kj = lax.broadcasted_iota(jnp.int32, (2 * blk, blk), 0)
    qi = lax.broadcasted_iota(jnp.int32, (2 * blk, blk), 1)
    in_window = (kj > qi) & (kj <= qi + blk)
    bias = jnp.where(in_window, 0.0, NEG_INF)
    bias_no_prev = jnp.where(in_window & (kj >= blk), 0.0, NEG_INF)
    bias_rest = jnp.concatenate([bias] * per_group, axis=1)
    bias_first = jnp.where(t == 0, jnp.concatenate([bias_no_prev] * per_group, axis=1), bias_rest)
    sink_row = lax.broadcasted_iota(jnp.int32, (2 * blk, per_group * blk), 0) == 0
    sinks = sink_ref[0]
    ones_blk = jnp.ones((2 * blk, LANES), BF16)
    first_key = lax.broadcasted_iota(jnp.int32, (2 * blk, LANES), 0) == 0

    units = [(i, g) for i in range(nblk) for g in range(ATTN_KV_HEADS)]
    logits = []
    for i, g in units:
        rows = slice(i * blk, (i + 1) * blk)
        parts = []
        for p in range(g * per_group // 2, (g + 1) * per_group // 2):
            qp = q[rows, p * LANES:(p + 1) * LANES]
            parts += [jnp.where(low_half, qp, 0.0), jnp.where(low_half, 0.0, qp)]
        qs = jnp.concatenate(parts, axis=0).astype(BF16)
        band = kd[g][i * blk:(i + 2) * blk]
        sink = jnp.concatenate(
            [jnp.broadcast_to(sinks[g * per_group + r:g * per_group + r + 1, :], (2 * blk, LANES))
             for r in range(per_group)], axis=1)
        lg = _dot_nt(band, qs) + (bias_first if i == 0 else bias_rest)
        logits.append(jnp.where(sink_row, sink, lg))
    expd = []
    for lg in logits:
        m = jnp.max(lg, axis=0, keepdims=True)
        expd.append(jnp.exp(lg - m).astype(BF16))
    for (i, g), e in zip(units, expd):
        vals = jnp.where(first_key, 0.0, vd[g][i * blk:(i + 2) * blk].astype(F32)).astype(BF16)
        o = _dot_tn(e, jnp.concatenate([vals, ones_blk], axis=1))
        o = o[:, :LANES] * (1.0 / o[:, LANES:])
        for j in range(per_group // 2):
            p = g * per_group // 2 + j
            pair = jnp.where(low_half, o[2 * j * blk:(2 * j + 1) * blk], o[(2 * j + 1) * blk:(2 * j + 2) * blk])
            o_ref[0, i * blk:(i + 1) * blk, p * LANES:(p + 1) * LANES] = pair.astype(BF16)


def _attn_call(proj3, layer, tabs, gains, bd, perm, sinks):
    nblk = ATTN_TILE // ATTN_BLOCK
    col = P_ATTN // ATTN_W
    tab_cur = pl.BlockSpec((ATTN_TILE, LANES), lambda b, t: (t, 0))
    tab_prev = pl.BlockSpec((ATTN_BLOCK, LANES), lambda b, t: (jnp.maximum(t * nblk - 1, 0), 0))
    return pl.pallas_call(
        _attn_kernel,
        grid=(BATCH, SEQ // ATTN_TILE),
        in_specs=[
            pl.BlockSpec((1, ATTN_TILE, ATTN_W), lambda b, t: (b, t, col)),
            pl.BlockSpec((1, ATTN_BLOCK, ATTN_W), lambda b, t: (b, jnp.maximum(t * nblk - 1, 0), col)),
            tab_cur, tab_cur, tab_prev, tab_prev,
            _layer_resident((1, ATTN_QK), layer),
            _resident((MXU_TILE, MXU_TILE)),
            _resident((MXU_TILE, MXU_TILE)),
            _layer_resident((ATTN_Q_HEADS, LANES), layer),
        ],
        out_specs=pl.BlockSpec((1, ATTN_TILE, 512), lambda b, t: (b, t, 0)),
        out_shape=jax.ShapeDtypeStruct((BATCH, SEQ, 512), BF16),
        compiler_params=_cparams("parallel", "parallel"),
        name="swa_attention",
    )(proj3, proj3, tabs[0], tabs[1], tabs[0], tabs[1], gains, bd, perm, sinks)


def _softplus(x):
    return jnp.maximum(x, 0.0) + jnp.log(1.0 + jnp.exp(-jnp.abs(x)))


def _dn_kernel(cur_ref, prev_ref, z_ref, ab_ref, abt_ref, conv_ref, alog_ref, dtb_ref,
               alogt_ref, dtbt_ref, onorm_ref, bd_ref, o_ref, state_ref, y_ref, c_ref, r_ref):
    t = pl.program_id(1)
    T = SEQ_TILE
    C = DN_CHUNK
    nh = DN_HEADS
    width = nh * DN_DIM

    @pl.when(t == 0)
    def _():
        state_ref[...] = jnp.zeros_like(state_ref)

    x = cur_ref[0].astype(F32)
    tail = jnp.where(t > 0, prev_ref[0].astype(F32)[SUBLANES:], 0.0)
    xc = jnp.concatenate([tail, x], axis=0)
    w = conv_ref[0]
    y = xc[SUBLANES:SUBLANES + T] * w[3:4]
    for j in range(DN_CONV - 1):
        off = SUBLANES - (DN_CONV - 1) + j
        y = y + xc[off:off + T] * w[j:j + 1]
    y = _silu(y)
    bd = bd_ref[...]
    q = y[:, :width]
    k = y[:, width:2 * width]
    v = y[:, 2 * width:]
    q = q * lax.rsqrt(_seg_sum(q * q, bd) + EPS) * (DN_DIM ** -0.5)
    k = k * lax.rsqrt(_seg_sum(k * k, bd) + EPS)

    ab = ab_ref[...]
    la = -jnp.exp(alog_ref[0]) * _softplus(ab + dtb_ref[0])
    beta = _sigmoid(ab)
    lat = -jnp.exp(alogt_ref[0]) * _softplus(abt_ref[0] + dtbt_ref[0])
    ri = lax.broadcasted_iota(jnp.int32, (T, T), 0)
    ci = lax.broadcasted_iota(jnp.int32, (T, T), 1)
    same = (ri // C) == (ci // C)
    causal = same & (ci <= ri)
    strict = same & (ci < ri)
    eye = (ri == ci).astype(F32)
    g_nat = _dot_exact_rhs(causal, la)
    g_t = _dot_exact_lhs(lat, same & (ri <= ci))
    gend_nat = jnp.concatenate(
        [jnp.broadcast_to(g_nat[c * C + C - 1:c * C + C, :], (C, LANES)) for c in range(T // C)], axis=0)

    low_half = lax.broadcasted_iota(jnp.int32, (1, LANES), 1) < DN_DIM
    high_half = jnp.logical_not(low_half)
    bdr = lax.broadcasted_iota(jnp.int32, (LANES, LANES), 0) // DN_DIM
    bdc = lax.broadcasted_iota(jnp.int32, (LANES, LANES), 1) // DN_DIM
    blockdiag = bdr == bdc

    def per_head_lanes(src, first):
        return jnp.where(low_half, src[:, first:first + 1], src[:, first + 1:first + 2])

    npair = nh // 2
    halves = (low_half, high_half)
    qd, kt, dec_end, rhs, pw, xinv, intras = [], [], [], [], [], [], []
    for p in range(npair):
        sl = slice(p * LANES, (p + 1) * LANES)
        gx = per_head_lanes(g_nat, 2 * p)
        gendx = per_head_lanes(gend_nat, 2 * p)
        bx = per_head_lanes(beta, nh + 2 * p)
        eg = jnp.exp(gx)
        k_p = k[:, sl]
        q_p = q[:, sl]
        kb = k_p * bx
        vb = v[:, sl] * bx
        kbe = kb * eg
        qd.append((q_p * eg).astype(BF16))
        kt.append((k_p * jnp.exp(gendx - gx)).astype(BF16))
        dec_end.append(jnp.exp(gendx))
        k_pb = k_p.astype(BF16)
        for half in range(2):
            h = 2 * p + half
            mh = halves[half]
            diff = g_nat[:, h:h + 1] - g_t[h:h + 1, :]
            decay = jnp.exp(jnp.where(causal, diff, NEG_INF))
            kk = _dot_nt(jnp.where(mh, kb, 0.0).astype(BF16), k_pb)
            qkm = _dot_nt(jnp.where(mh, q_p, 0.0).astype(BF16), k_pb)
            intras.append((qkm * decay).astype(BF16))
            n1 = jnp.where(strict, -(kk * decay), 0.0).astype(BF16)
            pw.append(n1)
            xinv.append(eye + n1.astype(F32))
            rhs.append(jnp.concatenate([jnp.where(mh, vb, 0.0), jnp.where(mh, kbe, 0.0)], axis=1))
    n1s = list(pw)
    for _ in range(5):
        for h in range(nh):
            pw[h] = _dot(pw[h], pw[h]).astype(BF16)
            xinv[h] = xinv[h] + _dot(xinv[h].astype(BF16), pw[h])
    xb = [xinv[h].astype(BF16) for h in range(nh)]
    one = bd[0:1, 0:1].astype(F32)
    for h in range(nh):
        y_ref[h] = _dot(xb[h], rhs[h].astype(BF16))
    for h in range(nh):
        yh, yl = _split_bf16(y_ref[h])
        c_ref[h] = _dot(n1s[h], yh) + _dot(n1s[h], yl)
    for h in range(nh):
        r_ref[h] = ((rhs[h] - y_ref[h]) + c_ref[h]).astype(BF16)
    for h in range(nh):
        c_ref[h] = _dot(xb[h], r_ref[h])
    for h in range(nh):
        y_ref[h] = y_ref[h] + one * c_ref[h]
    u, w_ = [], []
    for p in range(npair):
        u.append(y_ref[2 * p, :, :LANES] + y_ref[2 * p + 1, :, :LANES])
        w_.append((y_ref[2 * p, :, LANES:] + y_ref[2 * p + 1, :, LANES:]).astype(BF16))
    state = [state_ref[p] for p in range(npair)]
    vn = [[] for _ in range(npair)]
    oi = [[] for _ in range(npair)]
    for c in range(T // C):
        rows = slice(c * C, (c + 1) * C)
        for p in range(npair):
            sb = state[p].astype(BF16)
            vn_c = u[p][rows] - _dot(w_[p][rows], sb)
            oi[p].append(_dot(qd[p][rows], sb))
            upd = _dot_tn(kt[p][rows], vn_c.astype(BF16))
            state[p] = state[p] * dec_end[p][c * C:c * C + 1] + jnp.where(blockdiag, upd, 0.0)
            vn[p].append(vn_c)
    outs = []
    for p in range(npair):
        state_ref[p] = state[p]
        vn_all = jnp.concatenate(vn[p], axis=0)
        o_p = jnp.concatenate(oi[p], axis=0)
        for half in range(2):
            o_p = o_p + _dot(intras[2 * p + half], jnp.where(halves[half], vn_all, 0.0).astype(BF16))
        outs.append(o_p)
    o = jnp.concatenate(outs, axis=1)
    ms = _seg_sum(o * o, bd) * (1.0 / DN_DIM)
    zg = _silu(z_ref[0].astype(F32))
    o_ref[0] = (o * lax.rsqrt(ms + EPS) * onorm_ref[0] * zg).astype(BF16)


def _dn_call(proj3, ab, abt, layer, conv_w, alog, dtb, alog_t, dtb_t, onorm, bd):
    T = SEQ_TILE
    tail_rows = 2 * SUBLANES
    per_tile = T // tail_rows
    return pl.pallas_call(
        _dn_kernel,
        grid=(BATCH, SEQ // T),
        in_specs=[
            pl.BlockSpec((1, T, 1536), lambda b, t: (b, t, P_DQKV // 1536)),
            pl.BlockSpec((1, tail_rows, 1536), lambda b, t: (b, jnp.maximum(t * per_tile - 1, 0), P_DQKV // 1536)),
            pl.BlockSpec((1, T, 512), lambda b, t: (b, t, P_Z // 512)),
            pl.BlockSpec((T, LANES), lambda b, t: (b * (SEQ // T) + t, 0)),
            pl.BlockSpec((1, 2 * DN_HEADS, T), lambda b, t: (b, 0, t)),
            _layer_resident((DN_CONV, 1536), layer),
            _layer_resident((1, LANES), layer),
            _layer_resident((1, LANES), layer),
            _layer_resident((2 * DN_HEADS, T), layer),
            _layer_resident((2 * DN_HEADS, T), layer),
            _layer_resident((1, 512), layer),
            _resident((MXU_TILE, MXU_TILE)),
        ],
        out_specs=pl.BlockSpec((1, T, 512), lambda b, t: (b, t, 0)),
        out_shape=jax.ShapeDtypeStruct((BATCH, SEQ, 512), BF16),
        scratch_shapes=[pltpu.VMEM((DN_HEADS // 2, LANES, LANES), F32),
                        pltpu.VMEM((DN_HEADS, T, 2 * LANES), F32),
                        pltpu.VMEM((DN_HEADS, T, 2 * LANES), F32),
                        pltpu.VMEM((DN_HEADS, T, 2 * LANES), BF16)],
        compiler_params=_cparams("parallel", "arbitrary"),
        name="gated_deltanet",
    )(proj3, proj3, proj3, ab, abt, conv_w, alog, dtb, alog_t, dtb_t, onorm, bd)


def _ret_kernel(x_ref, tc_ref, ts_ref, perm_ref, o_ref, state_ref):
    t = pl.program_id(1)
    T = RET_TILE

    @pl.when(t == 0)
    def _():
        state_ref[...] = jnp.zeros_like(state_ref)

    x = x_ref[0]
    qk = _rotary(x[:, :512].astype(F32), perm_ref[...], tc_ref[...], ts_ref[...])
    rq = qk[:, :256]
    rk = qk[:, 256:] * (RET_KEY_DIM ** -0.5)
    ri = lax.broadcasted_iota(jnp.int32, (T, T), 0)
    ci = lax.broadcasted_iota(jnp.int32, (T, T), 1)
    causal = ci <= ri
    dist = (ri - ci).astype(F32)
    row = lax.broadcasted_iota(jnp.int32, (T, LANES), 0).astype(F32)
    low_half = lax.broadcasted_iota(jnp.int32, (1, LANES), 1) < RET_KEY_DIM

    log_gamma = [math.log1p(-(2.0 ** (-5.0 - h))) for h in range(RET_HEADS)]
    v = [x[:, 512 + h * LANES:512 + (h + 1) * LANES] for h in range(RET_HEADS)]
    scores, q_dec, k_end = [], [], []
    for h in range(RET_HEADS):
        p = h // 2
        mh = low_half if h % 2 == 0 else jnp.logical_not(low_half)
        q_p = rq[:, p * LANES:(p + 1) * LANES]
        k_p = rk[:, p * LANES:(p + 1) * LANES]
        decay = jnp.exp(jnp.where(causal, dist * log_gamma[h], NEG_INF))
        scores.append((_dot_nt(jnp.where(mh, q_p, 0.0).astype(BF16), k_p.astype(BF16)) * decay).astype(BF16))
        q_dec.append(jnp.where(mh, q_p * jnp.exp(log_gamma[h] * (row + 1.0)), 0.0).astype(BF16))
        k_end.append(jnp.where(mh, k_p * jnp.exp(log_gamma[h] * (T - 1.0 - row)), 0.0).astype(BF16))
    outs = []
    for h in range(RET_HEADS):
        state = state_ref[h]
        outs.append(_dot(scores[h], v[h]) + _dot(q_dec[h], state.astype(BF16)))
        state_ref[h] = state * math.exp(log_gamma[h] * T) + _dot_tn(k_end[h], v[h])
    for h in range(RET_HEADS):
        out = outs[h]
        ms = jnp.mean(out * out, axis=-1, keepdims=True)
        g = _silu(x[:, 1024 + h * LANES:1024 + (h + 1) * LANES].astype(F32))
        o_ref[0, :, h * LANES:(h + 1) * LANES] = (out * lax.rsqrt(ms + EPS) * g).astype(BF16)


def _ret_call(proj3, tabs, perm):
    T = RET_TILE
    tab = pl.BlockSpec((T, LANES), lambda b, t: (t, 0))
    return pl.pallas_call(
        _ret_kernel,
        grid=(BATCH, SEQ // T),
        in_specs=[pl.BlockSpec((1, T, 1536), lambda b, t: (b, t, P_RET // 1536)), tab, tab,
                  _resident((MXU_TILE, MXU_TILE))],
        out_specs=pl.BlockSpec((1, T, 512), lambda b, t: (b, t, 0)),
        out_shape=jax.ShapeDtypeStruct((BATCH, SEQ, 512), BF16),
        scratch_shapes=[pltpu.VMEM((RET_HEADS, LANES, LANES), F32)],
        compiler_params=_cparams("parallel", "arbitrary"),
        name="retention",
    )(proj3, tabs[0], tabs[1], perm)


def _merge_ffn_kernel(x_ref, mod_mix_ref, mod_ffn_ref, a_ref, b_ref, c_ref, g_ref, wb_ref, wo_ref,
                      gain_ref, w13_ref, w2_ref, o_ref, h_ref):
    merged = jnp.zeros((TM, D_MODEL), F32)
    for i, br in enumerate((a_ref, b_ref, c_ref)):
        gate = _sigmoid(g_ref[:, i * D_MODEL:(i + 1) * D_MODEL].astype(F32))
        merged = merged + gate * _dot(br[...], wb_ref[0, i])
    y = _dot(merged.astype(BF16), wo_ref[0])
    x = x_ref[...] + mod_mix_ref[0, 0, 0][2:3] * y
    o_ref[...] = _ffn_half_step(x, mod_ffn_ref[0, 0, 0], gain_ref[0], w13_ref, w2_ref, h_ref)


def _merge_ffn_call(x2d, mod5, layer, out_a, out_b, out_c, proj2, w_branch, w_out, gains, w13, w2):
    tiles_per_batch = SEQ // TM
    batch_of = lambda i: i // tiles_per_batch
    branch = pl.BlockSpec((TM, 512), lambda i: (i, 0))
    return pl.pallas_call(
        _merge_ffn_kernel,
        grid=(N_TOK // TM,),
        in_specs=[
            pl.BlockSpec((TM, D_MODEL), lambda i: (i, 0)),
            _mod_spec(layer, 1, batch_of),
            _mod_spec(layer, 2, batch_of),
            branch, branch, branch,
            pl.BlockSpec((TM, 3 * D_MODEL), lambda i: (i, P_GATES // (3 * D_MODEL))),
            _layer_resident((3, 512, D_MODEL), layer),
            _layer_resident((D_MODEL, D_MODEL), layer),
            _layer_resident((1, D_MODEL), layer),
            _layer_resident((D_MODEL, 2 * D_FF), layer),
            _layer_resident((D_FF, D_MODEL), layer),
        ],
        out_specs=pl.BlockSpec((TM, D_MODEL), lambda i: (i, 0)),
        out_shape=jax.ShapeDtypeStruct((N_TOK, D_MODEL), F32),
        scratch_shapes=[pltpu.VMEM((TM, D_FF), BF16)],
        compiler_params=_cparams("parallel"),
        name="merge_ffn",
    )(x2d, mod5, mod5, out_a, out_b, out_c, proj2, w_branch, w_out, gains, w13, w2)


def _block_diag_ones(n, seg):
    idx = np.arange(n) // seg
    return jnp.asarray((idx[:, None] == idx[None, :]).astype(np.float32), dtype=BF16)


def _partner_matrix(partner_of):
    m = np.zeros((MXU_TILE, MXU_TILE), np.float32)
    for dst in range(MXU_TILE):
        src = partner_of(dst)
        if src is not None:
            m[src, dst] = 1.0
    return jnp.asarray(m, dtype=BF16)


def _rope_partner(dst):
    d = dst % ATTN_HEAD_DIM
    half = ROPE_DIM // 2
    if d < half:
        return dst + half
    return dst - half if d < ROPE_DIM else None


def _attn_tables():
    half = ROPE_DIM // 2
    pos = jnp.arange(SEQ, dtype=F32)
    inv_freq = ROPE_THETA ** (-jnp.arange(0, ROPE_DIM, 2, dtype=F32) / ROPE_DIM)
    phase = pos[:, None] * inv_freq[None, :]
    cos, sin = jnp.cos(phase), jnp.sin(phase)
    pad = ATTN_HEAD_DIM - ROPE_DIM
    c = jnp.concatenate([cos, cos, jnp.ones((SEQ, pad), F32)], axis=1)
    s = jnp.concatenate([-sin, sin, jnp.zeros((SEQ, pad), F32)], axis=1)
    return jnp.tile(c, (1, 2)), jnp.tile(s, (1, 2))


def _ret_tables():
    pos = jnp.arange(SEQ, dtype=F32)
    angle = 1.0 / (RET_THETA ** jnp.linspace(0.0, 1.0, RET_KEY_DIM // 2, dtype=F32))
    angle = jnp.repeat(angle, 2)
    phase = pos[:, None] * angle[None, :]
    cos, sin = jnp.cos(phase), jnp.sin(phase)
    even = (jnp.arange(RET_KEY_DIM) % 2 == 0)[None, :]
    return jnp.tile(cos, (1, 2)), jnp.tile(jnp.where(even, -sin, sin), (1, 2))


W_IN_ROWS = 128
N_IN = 7440


def _w_in_relayout_kernel(w_ref, o_ref):
    w = w_ref[0]
    cs = lambda a, b: w[:, a:b]
    pad = jnp.zeros((W_IN_ROWS, P_ATTN - P_Z - 512), F32)
    cols = [cs(4368, 7440), cs(2832, 4368), cs(768, 2304), cs(2320, 2832), pad, cs(0, 768)]
    o_ref[0] = jnp.concatenate(cols, axis=1).astype(BF16)


def _relayout_w_in(w):
    w_p = pl.pallas_call(
        _w_in_relayout_kernel,
        grid=(DEPTH, D_MODEL // W_IN_ROWS),
        in_specs=[pl.BlockSpec((1, W_IN_ROWS, N_IN), lambda l, i: (l, i, 0))],
        out_specs=pl.BlockSpec((1, W_IN_ROWS, N_PROJ), lambda l, i: (l, i, 0)),
        out_shape=jax.ShapeDtypeStruct((DEPTH, D_MODEL, N_PROJ), BF16),
        compiler_params=_cparams("parallel", "parallel"),
        name="w_in_relayout",
    )(w)
    ab = w[:, :, 2304:2320]
    w_ab = jnp.concatenate([ab, jnp.zeros((DEPTH, D_MODEL, LANES - 2 * DN_HEADS), F32)], axis=2).astype(BF16)
    return w_p, w_ab, jnp.transpose(ab, (0, 2, 1)).astype(BF16)


def kernel(x, c, w_mod, b_mod, ffn1_norm, ffn1_w13, ffn1_w2, mix_norm, w_in, attn_q_norm, attn_k_norm,
           attn_sinks, dn_conv, dn_a_log, dn_dt_bias, dn_out_norm, w_branch, w_out, ffn2_norm, ffn2_w13,
           ffn2_w2):
    mod5 = _mod_call(c, w_mod, b_mod).reshape(DEPTH, BATCH, 3, 3, D_MODEL)
    attn_tabs = _attn_tables()
    ret_tabs = _ret_tables()
    bd = _block_diag_ones(MXU_TILE, ATTN_HEAD_DIM)
    rope_perm = _partner_matrix(_rope_partner)
    ret_perm = _partner_matrix(lambda dst: dst + 1 if dst % 2 == 0 else dst - 1)

    w13_1, w2_1 = ffn1_w13.astype(BF16), ffn1_w2.astype(BF16)
    w13_2, w2_2 = ffn2_w13.astype(BF16), ffn2_w2.astype(BF16)
    w_in_p, w_ab, w_abt = _relayout_w_in(w_in)
    w_branch_b, w_out_b = w_branch.astype(BF16), w_out.astype(BF16)
    gain1, gain_mix, gain2 = (g.reshape(DEPTH, 1, D_MODEL) for g in (ffn1_norm, mix_norm, ffn2_norm))
    qk_gain = jnp.concatenate([jnp.tile(attn_q_norm, (1, ATTN_Q_HEADS)),
                               jnp.tile(attn_k_norm, (1, ATTN_KV_HEADS))], axis=1)[:, None, :]
    sinks = jnp.broadcast_to(attn_sinks[:, :, None], (DEPTH, ATTN_Q_HEADS, LANES))
    lane_pad = ((0, 0), (0, LANES - DN_HEADS))
    alog = jnp.pad(dn_a_log, lane_pad)[:, None, :]
    dtb = jnp.pad(dn_dt_bias, lane_pad)[:, None, :]
    row_pad = ((0, 0), (0, DN_HEADS))
    alog_t = jnp.broadcast_to(jnp.pad(dn_a_log, row_pad)[:, :, None], (DEPTH, 2 * DN_HEADS, SEQ_TILE))
    dtb_t = jnp.broadcast_to(jnp.pad(dn_dt_bias, row_pad)[:, :, None], (DEPTH, 2 * DN_HEADS, SEQ_TILE))
    onorm = jnp.tile(dn_out_norm, (1, DN_HEADS))[:, None, :]

    xf = x.reshape(N_TOK, D_MODEL)
    for l in range(DEPTH):
        xf = _ffn_call(xf, mod5, l, 0, gain1, w13_1, w2_1)
        proj, ab, abt = _inproj_call(xf, mod5, l, gain_mix, w_in_p, w_ab, w_abt)
        proj3 = proj.reshape(BATCH, SEQ, N_PROJ)
        out_a = _attn_call(proj3, l, attn_tabs, qk_gain, bd, rope_perm, sinks)
        out_b = _dn_call(proj3, ab, abt, l, dn_conv, alog, dtb, alog_t, dtb_t, onorm, bd)
        out_c = _ret_call(proj3, ret_tabs, ret_perm)
        xf = _merge_ffn_call(xf, mod5, l, out_a.reshape(N_TOK, 512), out_b.reshape(N_TOK, 512),
                             out_c.reshape(N_TOK, 512), proj, w_branch_b, w_out_b, gain2, w13_2, w2_2)
    return xf.reshape(BATCH, SEQ, D_MODEL)
```

```python
import math

import numpy as np
import jax
import jax.numpy as jnp
from jax import lax
from jax.experimental import pallas as pl
from jax.experimental.pallas import tpu as pltpu

F32 = jnp.float32
BF16 = jnp.bfloat16
HIGHEST = lax.Precision.HIGHEST

D_MODEL = 1024
BATCH = 8
SEQ = 2048
DEPTH = 2
N_TOK = BATCH * SEQ
N_MOD = 9
D_FF = 2816
EPS = 1e-6
NEG_INF = -1e30

ATTN_Q_HEADS = 8
ATTN_KV_HEADS = 2
ATTN_HEAD_DIM = 64
ATTN_BLOCK = 128
ROPE_DIM = 16
ROPE_THETA = 500000.0
DN_HEADS = 8
DN_DIM = 64
DN_CONV = 4
RET_HEADS = 4
RET_KEY_DIM = 64
RET_THETA = 10000.0

LANES = 128
SUBLANES = 8
MXU_TILE = 256

P_GATES = 0
P_RET = 3072
P_DQKV = 4608
P_Z = 6144
P_ATTN = 6912
N_PROJ = 7680
ATTN_W = 768
ATTN_QK = 640
W_HEAD = 2304
W_TAIL_START = 2320
W_TAIL = 5120
PROJ_PIECES = (
    (True, 2048, 3072, P_GATES),
    (True, 512, 1536, P_RET),
    (False, 768, 1536, P_DQKV),
    (True, 0, 512, P_Z),
    (False, 0, 768, P_ATTN),
)

TM = 512
FFN_CHUNK = 256
N_FFN_CHUNK = D_FF // FFN_CHUNK
PROJ_CHUNK = 768
MOD_TN = 1152
SEQ_TILE = 128
RET_TILE = 256
ATTN_TILE = 512
DN_CHUNK = 64
VMEM_LIMIT = 56 * 1024 * 1024


def _dot(a, b, precision=None):
    return jnp.dot(a, b, preferred_element_type=F32, precision=precision)


def _dot_nt(a, b):
    return lax.dot_general(a, b, (((1,), (1,)), ((), ())), preferred_element_type=F32)


def _dot_tn(a, b):
    return lax.dot_general(a, b, (((0,), (0,)), ((), ())), preferred_element_type=F32)


def _split_bf16(a):
    hi = a.astype(BF16)
    return hi, (a - hi.astype(F32)).astype(BF16)


def _split3_bf16(a):
    h1 = a.astype(BF16)
    r1 = a - h1.astype(F32)
    h2 = r1.astype(BF16)
    h3 = (r1 - h2.astype(F32)).astype(BF16)
    return h1, h2, h3


def _dot_exact_rhs(m01, x):
    m = m01.astype(BF16)
    h1, h2, h3 = _split3_bf16(x)
    return _dot(m, h1) + (_dot(m, h2) + _dot(m, h3))


def _dot_exact_lhs(x, m01):
    m = m01.astype(BF16)
    h1, h2, h3 = _split3_bf16(x)
    return _dot(h1, m) + (_dot(h2, m) + _dot(h3, m))


def _sigmoid(x):
    return 0.5 * (jnp.tanh(0.5 * x) + 1.0)


def _silu(x):
    return x * _sigmoid(x)


def _modulate(x, gain, shift, scale):
    ms = jnp.mean(x * x, axis=-1, keepdims=True)
    return x * lax.rsqrt(ms + EPS) * (gain * (1.0 + scale)) + shift


def _lane_block_matmul(pieces, m):
    blk = m.shape[0]
    width = pieces[0].shape[-1]
    outs = []
    for s in range(0, width, blk):
        n = min(blk, width - s)
        mm = m if n == blk else m[:n, :n]
        acc = _dot(pieces[0][:, s:s + n], mm)
        for piece in pieces[1:]:
            acc = acc + _dot(piece[:, s:s + n], mm)
        outs.append(acc)
    return outs[0] if len(outs) == 1 else jnp.concatenate(outs, axis=1)


def _seg_sum(x2, bd):
    return _lane_block_matmul([x2.astype(BF16)], bd)


def _tile_lanes(t, width):
    reps = width // t.shape[-1]
    return t if reps == 1 else jnp.concatenate([t] * reps, axis=1)


def _rotary(x, perm, cos, sin):
    w = x.shape[-1]
    return x * _tile_lanes(cos, w) + _lane_block_matmul(list(_split_bf16(x)), perm) * _tile_lanes(sin, w)


def _cparams(*sem):
    return pltpu.CompilerParams(dimension_semantics=sem, vmem_limit_bytes=VMEM_LIMIT)


def _resident(shape):
    zeros = (0,) * len(shape)
    return pl.BlockSpec(shape, lambda *_: zeros, pipeline_mode=pl.Buffered(1))


def _layer_resident(shape, layer):
    idx = (layer,) + (0,) * len(shape)
    return pl.BlockSpec((1,) + tuple(shape), lambda *_: idx, pipeline_mode=pl.Buffered(1))


def _mod_spec(layer, group, batch_of):
    return pl.BlockSpec((1, 1, 1, 3, D_MODEL), lambda *g: (layer, batch_of(*g), group, 0, 0))


def _mod_kernel(c_ref, w_ref, b_ref, o_ref):
    c = c_ref[...]
    o_ref[0] = _dot(_silu(c), w_ref[0], precision=HIGHEST) + b_ref[0]


def _mod_call(c, w_mod, b_mod):
    n = N_MOD * D_MODEL
    return pl.pallas_call(
        _mod_kernel,
        grid=(DEPTH, n // MOD_TN),
        in_specs=[
            pl.BlockSpec((BATCH, D_MODEL), lambda l, j: (0, 0)),
            pl.BlockSpec((1, D_MODEL, MOD_TN), lambda l, j: (l, 0, j)),
            pl.BlockSpec((1, 1, MOD_TN), lambda l, j: (l, 0, j)),
        ],
        out_specs=pl.BlockSpec((1, BATCH, MOD_TN), lambda l, j: (l, 0, j)),
        out_shape=jax.ShapeDtypeStruct((DEPTH, BATCH, n), F32),
        compiler_params=_cparams("parallel", "parallel"),
        name="adaln_mod",
    )(c, w_mod, b_mod.reshape(DEPTH, 1, n))


def _ffn_half_step(x, mod, gain, w13_ref, w2_ref, h_ref):
    ub = _modulate(x, gain, mod[0:1], mod[1:2]).astype(BF16)
    for j in range(N_FFN_CHUNK):
        lo = j * FFN_CHUNK
        g = _dot(ub, w13_ref[0, :, lo:lo + FFN_CHUNK])
        up = _dot(ub, w13_ref[0, :, D_FF + lo:D_FF + lo + FFN_CHUNK])
        h_ref[:, lo:lo + FFN_CHUNK] = (_silu(g) * up).astype(BF16)
    y = _dot(h_ref[...], w2_ref[0])
    return x + (0.5 * mod[2:3]) * y


def _ffn_kernel(x_ref, mod_ref, gain_ref, w13_ref, w2_ref, o_ref, h_ref):
    o_ref[...] = _ffn_half_step(x_ref[...], mod_ref[0, 0, 0], gain_ref[0], w13_ref, w2_ref, h_ref)


def _ffn_call(x2d, mod5, layer, group, gains, w13, w2):
    tiles_per_batch = SEQ // TM
    return pl.pallas_call(
        _ffn_kernel,
        grid=(N_TOK // TM,),
        in_specs=[
            pl.BlockSpec((TM, D_MODEL), lambda i: (i, 0)),
            _mod_spec(layer, group, lambda i: i // tiles_per_batch),
            _layer_resident((1, D_MODEL), layer),
            _layer_resident((D_MODEL, 2 * D_FF), layer),
            _layer_resident((D_FF, D_MODEL), layer),
        ],
        out_specs=pl.BlockSpec((TM, D_MODEL), lambda i: (i, 0)),
        out_shape=jax.ShapeDtypeStruct((N_TOK, D_MODEL), F32),
        scratch_shapes=[pltpu.VMEM((TM, D_FF), BF16)],
        compiler_params=_cparams("parallel"),
        name="swiglu_ffn",
    )(x2d, mod5, gains, w13, w2)


def _inproj_kernel(x_ref, mod_ref, gain_ref, wh_ref, wt_ref, wab_ref, wabt_ref, proj_ref, ab_ref, abt_ref):
    mod = mod_ref[0, 0, 0]
    ub = _modulate(x_ref[...], gain_ref[0], mod[0:1], mod[1:2]).astype(BF16)
    for from_tail, src, width, dst in PROJ_PIECES:
        w_ref = wt_ref if from_tail else wh_ref
        for off in range(0, width, PROJ_CHUNK):
            n = min(PROJ_CHUNK, width - off)
            proj_ref[:, dst + off:dst + off + n] = _dot(ub, w_ref[0, :, src + off:src + off + n]).astype(BF16)
    proj_ref[:, P_Z + 512:P_ATTN] = jnp.zeros((TM, P_ATTN - P_Z - 512), BF16)
    ab_ref[...] = _dot(ub, wab_ref[0])
    abt_ref[0] = _dot_nt(wabt_ref[0], ub)


def _inproj_call(x2d, mod5, layer, gains, w_head, w_tail, w_ab, w_abt):
    tiles_per_batch = SEQ // TM
    return pl.pallas_call(
        _inproj_kernel,
        grid=(N_TOK // TM,),
        in_specs=[
            pl.BlockSpec((TM, D_MODEL), lambda i: (i, 0)),
            _mod_spec(layer, 1, lambda i: i // tiles_per_batch),
            _layer_resident((1, D_MODEL), layer),
            _layer_resident((D_MODEL, W_HEAD), layer),
            _layer_resident((D_MODEL, W_TAIL), layer),
            _layer_resident((D_MODEL, LANES), layer),
            _layer_resident((2 * DN_HEADS, D_MODEL), layer),
        ],
        out_specs=[
            pl.BlockSpec((TM, N_PROJ), lambda i: (i, 0)),
            pl.BlockSpec((TM, LANES), lambda i: (i, 0)),
            pl.BlockSpec((1, 2 * DN_HEADS, TM), lambda i: (i // tiles_per_batch, 0, i % tiles_per_batch)),
        ],
        out_shape=[
            jax.ShapeDtypeStruct((N_TOK, N_PROJ), BF16),
            jax.ShapeDtypeStruct((N_TOK, LANES), F32),
            jax.ShapeDtypeStruct((BATCH, 2 * DN_HEADS, SEQ), F32),
        ],
        compiler_params=_cparams("parallel"),
        name="mixer_in_proj",
    )(x2d, mod5, gains, w_head, w_tail, w_ab, w_abt)


def _attn_kernel(cur_ref, prev_ref, tc_ref, ts_ref, pc_ref, ps_ref, gain_ref, bd_ref, perm_ref,
                 sink_ref, o_ref):
    t = pl.program_id(1)
    blk = ATTN_BLOCK
    nblk = ATTN_TILE // blk
    per_group = ATTN_Q_HEADS // ATTN_KV_HEADS
    bd = bd_ref[...]
    perm = perm_ref[...]
    gain = gain_ref[0]

    def norm_rope(x, g, cos, sin):
        x = x * lax.rsqrt(_seg_sum(x * x, bd) * (1.0 / ATTN_HEAD_DIM) + EPS) * g
        return _rotary(x, perm, cos, sin)

    cur = cur_ref[0]
    prv = prev_ref[0]
    qk = norm_rope(cur[:, :ATTN_QK].astype(F32), gain, tc_ref[...], ts_ref[...])
    kprev = norm_rope(prv[:, 512:ATTN_QK].astype(F32), gain[:, 512:], pc_ref[...], ps_ref[...])
    q = qk[:, :512] * (ATTN_HEAD_DIM ** -0.5)
    k_all = jnp.concatenate([kprev, qk[:, 512:]], axis=0)
    v_all = jnp.concatenate([prv[:, ATTN_QK:], cur[:, ATTN_QK:]], axis=0).astype(F32)

    low_half = lax.broadcasted_iota(jnp.int32, (1, LANES), 1) < ATTN_HEAD_DIM

    def group_on_both_halves(a, g):
        swapped = pltpu.roll(a, ATTN_HEAD_DIM, 1)
        return (jnp.where(low_half, a, swapped) if g == 0 else jnp.where(low_half, swapped, a)).astype(BF16)

    kd = [group_on_both_halves(k_all, g) for g in range(ATTN_KV_HEADS)]
    vd = [group_on_both_halves(v_all, g) for g in range(ATTN_KV_HEADS)]

    kj = lax.broadcasted_iota(jnp.int32, (2 * blk, blk), 0)
    qi = lax.broadcasted_iota(jnp.int32, (2 * blk, blk), 1)
    in_window = (kj > qi) & (kj <= qi + blk)
    bias = jnp.where(in_window, 0.0, NEG_INF)
    bias_no_prev = jnp.where(in_window & (kj >= blk), 0.0, NEG_INF)
    bias_rest = jnp.concatenate([bias] * per_group, axis=1)
    bias_first = jnp.where(t == 0, jnp.concatenate([bias_no_prev] * per_group, axis=1), bias_rest)
    sink_row = lax.broadcasted_iota(jnp.int32, (2 * blk, per_group * blk), 0) == 0
    sinks = sink_ref[0]
    ones_blk = jnp.ones((2 * blk, LANES), BF16)
    first_key = lax.broadcasted_iota(jnp.int32, (2 * blk, LANES), 0) == 0

    units = [(i, g) for i in range(nblk) for g in range(ATTN_KV_HEADS)]
    logits = []
    for i, g in units:
        rows = slice(i * blk, (i + 1) * blk)
        parts = []
        for p in range(g * per_group // 2, (g + 1) * per_group // 2):
            qp = q[rows, p * LANES:(p + 1) * LANES]
            parts += [jnp.where(low_half, qp, 0.0), jnp.where(low_half, 0.0, qp)]
        qs = jnp.concatenate(parts, axis=0).astype(BF16)
        band = kd[g][i * blk:(i + 2) * blk]
        sink = jnp.concatenate(
            [jnp.broadcast_to(sinks[g * per_group + r:g * per_group + r + 1, :], (2 * blk, LANES))
             for r in range(per_group)], axis=1)
        lg = _dot_nt(band, qs) + (bias_first if i == 0 else bias_rest)
        logits.append(jnp.where(sink_row, sink, lg))
    expd = []
    for lg in logits:
        m = jnp.max(lg, axis=0, keepdims=True)
        expd.append(jnp.exp(lg - m).astype(BF16))
    for (i, g), e in zip(units, expd):
        vals = jnp.where(first_key, 0.0, vd[g][i * blk:(i + 2) * blk].astype(F32)).astype(BF16)
        o = _dot_tn(e, jnp.concatenate([vals, ones_blk], axis=1))
        o = o[:, :LANES] * (1.0 / o[:, LANES:])
        for j in range(per_group // 2):
            p = g * per_group // 2 + j
            pair = jnp.where(low_half, o[2 * j * blk:(2 * j + 1) * blk], o[(2 * j + 1) * blk:(2 * j + 2) * blk])
            o_ref[0, i * blk:(i + 1) * blk, p * LANES:(p + 1) * LANES] = pair.astype(BF16)


def _attn_call(proj3, layer, tabs, gains, bd, perm, sinks):
    nblk = ATTN_TILE // ATTN_BLOCK
    col = P_ATTN // ATTN_W
    tab_cur = pl.BlockSpec((ATTN_TILE, LANES), lambda b, t: (t, 0))
    tab_prev = pl.BlockSpec((ATTN_BLOCK, LANES), lambda b, t: (jnp.maximum(t * nblk - 1, 0), 0))
    return pl.pallas_call(
        _attn_kernel,
        grid=(BATCH, SEQ // ATTN_TILE),
        in_specs=[
            pl.BlockSpec((1, ATTN_TILE, ATTN_W), lambda b, t: (b, t, col)),
            pl.BlockSpec((1, ATTN_BLOCK, ATTN_W), lambda b, t: (b, jnp.maximum(t * nblk - 1, 0), col)),
            tab_cur, tab_cur, tab_prev, tab_prev,
            _layer_resident((1, ATTN_QK), layer),
            _resident((MXU_TILE, MXU_TILE)),
            _resident((MXU_TILE, MXU_TILE)),
            _layer_resident((ATTN_Q_HEADS, LANES), layer),
        ],
        out_specs=pl.BlockSpec((1, ATTN_TILE, 512), lambda b, t: (b, t, 0)),
        out_shape=jax.ShapeDtypeStruct((BATCH, SEQ, 512), BF16),
        compiler_params=_cparams("parallel", "parallel"),
        name="swa_attention",
    )(proj3, proj3, tabs[0], tabs[1], tabs[0], tabs[1], gains, bd, perm, sinks)


def _softplus(x):
    return jnp.maximum(x, 0.0) + jnp.log(1.0 + jnp.exp(-jnp.abs(x)))


def _dn_kernel(cur_ref, prev_ref, z_ref, ab_ref, abt_ref, conv_ref, alog_ref, dtb_ref,
               alogt_ref, dtbt_ref, onorm_ref, bd_ref, o_ref, state_ref, y_ref, c_ref, r_ref):
    t = pl.program_id(1)
    T = SEQ_TILE
    C = DN_CHUNK
    nh = DN_HEADS
    width = nh * DN_DIM

    @pl.when(t == 0)
    def _():
        state_ref[...] = jnp.zeros_like(state_ref)

    x = cur_ref[0].astype(F32)
    tail = jnp.where(t > 0, prev_ref[0].astype(F32)[SUBLANES:], 0.0)
    xc = jnp.concatenate([tail, x], axis=0)
    w = conv_ref[0]
    y = xc[SUBLANES:SUBLANES + T] * w[3:4]
    for j in range(DN_CONV - 1):
        off = SUBLANES - (DN_CONV - 1) + j
        y = y + xc[off:off + T] * w[j:j + 1]
    y = _silu(y)
    bd = bd_ref[...]
    q = y[:, :width]
    k = y[:, width:2 * width]
    v = y[:, 2 * width:]
    q = q * lax.rsqrt(_seg_sum(q * q, bd) + EPS) * (DN_DIM ** -0.5)
    k = k * lax.rsqrt(_seg_sum(k * k, bd) + EPS)

    ab = ab_ref[...]
    la = -jnp.exp(alog_ref[0]) * _softplus(ab + dtb_ref[0])
    beta = _sigmoid(ab)
    lat = -jnp.exp(alogt_ref[0]) * _softplus(abt_ref[0] + dtbt_ref[0])
    ri = lax.broadcasted_iota(jnp.int32, (T, T), 0)
    ci = lax.broadcasted_iota(jnp.int32, (T, T), 1)
    same = (ri // C) == (ci // C)
    causal = same & (ci <= ri)
    strict = same & (ci < ri)
    eye = (ri == ci).astype(F32)
    g_nat = _dot_exact_rhs(causal, la)
    g_t = _dot_exact_lhs(lat, same & (ri <= ci))
    gend_nat = jnp.concatenate(
        [jnp.broadcast_to(g_nat[c * C + C - 1:c * C + C, :], (C, LANES)) for c in range(T // C)], axis=0)

    low_half = lax.broadcasted_iota(jnp.int32, (1, LANES), 1) < DN_DIM
    high_half = jnp.logical_not(low_half)
    bdr = lax.broadcasted_iota(jnp.int32, (LANES, LANES), 0) // DN_DIM
    bdc = lax.broadcasted_iota(jnp.int32, (LANES, LANES), 1) // DN_DIM
    blockdiag = bdr == bdc

    def per_head_lanes(src, first):
        return jnp.where(low_half, src[:, first:first + 1], src[:, first + 1:first + 2])

    npair = nh // 2
    halves = (low_half, high_half)
    qd, kt, dec_end, rhs, pw, xinv, intras = [], [], [], [], [], [], []
    for p in range(npair):
        sl = slice(p * LANES, (p + 1) * LANES)
        gx = per_head_lanes(g_nat, 2 * p)
        gendx = per_head_lanes(gend_nat, 2 * p)
        bx = per_head_lanes(beta, nh + 2 * p)
        eg = jnp.exp(gx)
        k_p = k[:, sl]
        q_p = q[:, sl]
        kb = k_p * bx
        vb = v[:, sl] * bx
        kbe = kb * eg
        qd.append((q_p * eg).astype(BF16))
        kt.append((k_p * jnp.exp(gendx - gx)).astype(BF16))
        dec_end.append(jnp.exp(gendx))
        k_pb = k_p.astype(BF16)
        for half in range(2):
            h = 2 * p + half
            mh = halves[half]
            diff = g_nat[:, h:h + 1] - g_t[h:h + 1, :]
            decay = jnp.exp(jnp.where(causal, diff, NEG_INF))
            kk = _dot_nt(jnp.where(mh, kb, 0.0).astype(BF16), k_pb)
            qkm = _dot_nt(jnp.where(mh, q_p, 0.0).astype(BF16), k_pb)
            intras.append((qkm * decay).astype(BF16))
            n1 = jnp.where(strict, -(kk * decay), 0.0).astype(BF16)
            pw.append(n1)
            xinv.append(eye + n1.astype(F32))
            rhs.append(jnp.concatenate([jnp.where(mh, vb, 0.0), jnp.where(mh, kbe, 0.0)], axis=1))
    n1s = list(pw)
    for _ in range(5):
        for h in range(nh):
            pw[h] = _dot(pw[h], pw[h]).astype(BF16)
            xinv[h] = xinv[h] + _dot(xinv[h].astype(BF16), pw[h])
    xb = [xinv[h].astype(BF16) for h in range(nh)]
    one = bd[0:1, 0:1].astype(F32)
    for h in range(nh):
        y_ref[h] = _dot(xb[h], rhs[h].astype(BF16))
    for h in range(nh):
        yh, yl = _split_bf16(y_ref[h])
        c_ref[h] = _dot(n1s[h], yh) + _dot(n1s[h], yl)
    for h in range(nh):
        r_ref[h] = ((rhs[h] - y_ref[h]) + c_ref[h]).astype(BF16)
    for h in range(nh):
        c_ref[h] = _dot(xb[h], r_ref[h])
    for h in range(nh):
        y_ref[h] = y_ref[h] + one * c_ref[h]
    u, w_ = [], []
    for p in range(npair):
        u.append(y_ref[2 * p, :, :LANES] + y_ref[2 * p + 1, :, :LANES])
        w_.append((y_ref[2 * p, :, LANES:] + y_ref[2 * p + 1, :, LANES:]).astype(BF16))
    state = [state_ref[p] for p in range(npair)]
    vn = [[] for _ in range(npair)]
    oi = [[] for _ in range(npair)]
    for c in range(T // C):
        rows = slice(c * C, (c + 1) * C)
        for p in range(npair):
            sb = state[p].astype(BF16)
            vn_c = u[p][rows] - _dot(w_[p][rows], sb)
            oi[p].append(_dot(qd[p][rows], sb))
            upd = _dot_tn(kt[p][rows], vn_c.astype(BF16))
            state[p] = state[p] * dec_end[p][c * C:c * C + 1] + jnp.where(blockdiag, upd, 0.0)
            vn[p].append(vn_c)
    outs = []
    for p in range(npair):
        state_ref[p] = state[p]
        vn_all = jnp.concatenate(vn[p], axis=0)
        o_p = jnp.concatenate(oi[p], axis=0)
        for half in range(2):
            o_p = o_p + _dot(intras[2 * p + half], jnp.where(halves[half], vn_all, 0.0).astype(BF16))
        outs.append(o_p)
    o = jnp.concatenate(outs, axis=1)
    ms = _seg_sum(o * o, bd) * (1.0 / DN_DIM)
    zg = _silu(z_ref[0].astype(F32))
    o_ref[0] = (o * lax.rsqrt(ms + EPS) * onorm_ref[0] * zg).astype(BF16)


def _dn_call(proj3, ab, abt, layer, conv_w, alog, dtb, alog_t, dtb_t, onorm, bd):
    T = SEQ_TILE
    tail_rows = 2 * SUBLANES
    per_tile = T // tail_rows
    return pl.pallas_call(
        _dn_kernel,
        grid=(BATCH, SEQ // T),
        in_specs=[
            pl.BlockSpec((1, T, 1536), lambda b, t: (b, t, P_DQKV // 1536)),
            pl.BlockSpec((1, tail_rows, 1536), lambda b, t: (b, jnp.maximum(t * per_tile - 1, 0), P_DQKV // 1536)),
            pl.BlockSpec((1, T, 512), lambda b, t: (b, t, P_Z // 512)),
            pl.BlockSpec((T, LANES), lambda b, t: (b * (SEQ // T) + t, 0)),
            pl.BlockSpec((1, 2 * DN_HEADS, T), lambda b, t: (b, 0, t)),
            _layer_resident((DN_CONV, 1536), layer),
            _layer_resident((1, LANES), layer),
            _layer_resident((1, LANES), layer),
            _layer_resident((2 * DN_HEADS, T), layer),
            _layer_resident((2 * DN_HEADS, T), layer),
            _layer_resident((1, 512), layer),
            _resident((MXU_TILE, MXU_TILE)),
        ],
        out_specs=pl.BlockSpec((1, T, 512), lambda b, t: (b, t, 0)),
        out_shape=jax.ShapeDtypeStruct((BATCH, SEQ, 512), BF16),
        scratch_shapes=[pltpu.VMEM((DN_HEADS // 2, LANES, LANES), F32),
                        pltpu.VMEM((DN_HEADS, T, 2 * LANES), F32),
                        pltpu.VMEM((DN_HEADS, T, 2 * LANES), F32),
                        pltpu.VMEM((DN_HEADS, T, 2 * LANES), BF16)],
        compiler_params=_cparams("parallel", "arbitrary"),
        name="gated_deltanet",
    )(proj3, proj3, proj3, ab, abt, conv_w, alog, dtb, alog_t, dtb_t, onorm, bd)


def _ret_kernel(x_ref, tc_ref, ts_ref, perm_ref, o_ref, state_ref):
    t = pl.program_id(1)
    T = RET_TILE

    @pl.when(t == 0)
    def _():
        state_ref[...] = jnp.zeros_like(state_ref)

    x = x_ref[0]
    qk = _rotary(x[:, :512].astype(F32), perm_ref[...], tc_ref[...], ts_ref[...])
    rq = qk[:, :256]
    rk = qk[:, 256:] * (RET_KEY_DIM ** -0.5)
    ri = lax.broadcasted_iota(jnp.int32, (T, T), 0)
    ci = lax.broadcasted_iota(jnp.int32, (T, T), 1)
    causal = ci <= ri
    dist = (ri - ci).astype(F32)
    row = lax.broadcasted_iota(jnp.int32, (T, LANES), 0).astype(F32)
    low_half = lax.broadcasted_iota(jnp.int32, (1, LANES), 1) < RET_KEY_DIM

    log_gamma = [math.log1p(-(2.0 ** (-5.0 - h))) for h in range(RET_HEADS)]
    v = [x[:, 512 + h * LANES:512 + (h + 1) * LANES] for h in range(RET_HEADS)]
    scores, q_dec, k_end = [], [], []
    for h in range(RET_HEADS):
        p = h // 2
        mh = low_half if h % 2 == 0 else jnp.logical_not(low_half)
        q_p = rq[:, p * LANES:(p + 1) * LANES]
        k_p = rk[:, p * LANES:(p + 1) * LANES]
        decay = jnp.exp(jnp.where(causal, dist * log_gamma[h], NEG_INF))
        scores.append((_dot_nt(jnp.where(mh, q_p, 0.0).astype(BF16), k_p.astype(BF16)) * decay).astype(BF16))
        q_dec.append(jnp.where(mh, q_p * jnp.exp(log_gamma[h] * (row + 1.0)), 0.0).astype(BF16))
        k_end.append(jnp.where(mh, k_p * jnp.exp(log_gamma[h] * (T - 1.0 - row)), 0.0).astype(BF16))
    outs = []
    for h in range(RET_HEADS):
        state = state_ref[h]
        outs.append(_dot(scores[h], v[h]) + _dot(q_dec[h], state.astype(BF16)))
        state_ref[h] = state * math.exp(log_gamma[h] * T) + _dot_tn(k_end[h], v[h])
    for h in range(RET_HEADS):
        out = outs[h]
        ms = jnp.mean(out * out, axis=-1, keepdims=True)
        g = _silu(x[:, 1024 + h * LANES:1024 + (h + 1) * LANES].astype(F32))
        o_ref[0, :, h * LANES:(h + 1) * LANES] = (out * lax.rsqrt(ms + EPS) * g).astype(BF16)


def _ret_call(proj3, tabs, perm):
    T = RET_TILE
    tab = pl.BlockSpec((T, LANES), lambda b, t: (t, 0))
    return pl.pallas_call(
        _ret_kernel,
        grid=(BATCH, SEQ // T),
        in_specs=[pl.BlockSpec((1, T, 1536), lambda b, t: (b, t, P_RET // 1536)), tab, tab,
                  _resident((MXU_TILE, MXU_TILE))],
        out_specs=pl.BlockSpec((1, T, 512), lambda b, t: (b, t, 0)),
        out_shape=jax.ShapeDtypeStruct((BATCH, SEQ, 512), BF16),
        scratch_shapes=[pltpu.VMEM((RET_HEADS, LANES, LANES), F32)],
        compiler_params=_cparams("parallel", "arbitrary"),
        name="retention",
    )(proj3, tabs[0], tabs[1], perm)


def _merge_ffn_kernel(x_ref, mod_mix_ref, mod_ffn_ref, a_ref, b_ref, c_ref, g_ref, wb_ref, wo_ref,
                      gain_ref, w13_ref, w2_ref, o_ref, h_ref):
    merged = jnp.zeros((TM, D_MODEL), F32)
    for i, br in enumerate((a_ref, b_ref, c_ref)):
        gate = _sigmoid(g_ref[:, i * D_MODEL:(i + 1) * D_MODEL].astype(F32))
        merged = merged + gate * _dot(br[...], wb_ref[0, i])
    y = _dot(merged.astype(BF16), wo_ref[0])
    x = x_ref[...] + mod_mix_ref[0, 0, 0][2:3] * y
    o_ref[...] = _ffn_half_step(x, mod_ffn_ref[0, 0, 0], gain_ref[0], w13_ref, w2_ref, h_ref)


def _merge_ffn_call(x2d, mod5, layer, out_a, out_b, out_c, proj2, w_branch, w_out, gains, w13, w2):
    tiles_per_batch = SEQ // TM
    batch_of = lambda i: i // tiles_per_batch
    branch = pl.BlockSpec((TM, 512), lambda i: (i, 0))
    return pl.pallas_call(
        _merge_ffn_kernel,
        grid=(N_TOK // TM,),
        in_specs=[
            pl.BlockSpec((TM, D_MODEL), lambda i: (i, 0)),
            _mod_spec(layer, 1, batch_of),
            _mod_spec(layer, 2, batch_of),
            branch, branch, branch,
            pl.BlockSpec((TM, 3 * D_MODEL), lambda i: (i, P_GATES // (3 * D_MODEL))),
            _layer_resident((3, 512, D_MODEL), layer),
            _layer_resident((D_MODEL, D_MODEL), layer),
            _layer_resident((1, D_MODEL), layer),
            _layer_resident((D_MODEL, 2 * D_FF), layer),
            _layer_resident((D_FF, D_MODEL), layer),
        ],
        out_specs=pl.BlockSpec((TM, D_MODEL), lambda i: (i, 0)),
        out_shape=jax.ShapeDtypeStruct((N_TOK, D_MODEL), F32),
        scratch_shapes=[pltpu.VMEM((TM, D_FF), BF16)],
        compiler_params=_cparams("parallel"),
        name="merge_ffn",
    )(x2d, mod5, mod5, out_a, out_b, out_c, proj2, w_branch, w_out, gains, w13, w2)


def _block_diag_ones(n, seg):
    idx = np.arange(n) // seg
    return jnp.asarray((idx[:, None] == idx[None, :]).astype(np.float32), dtype=BF16)


def _partner_matrix(partner_of):
    m = np.zeros((MXU_TILE, MXU_TILE), np.float32)
    for dst in range(MXU_TILE):
        src = partner_of(dst)
        if src is not None:
            m[src, dst] = 1.0
    return jnp.asarray(m, dtype=BF16)


def _rope_partner(dst):
    d = dst % ATTN_HEAD_DIM
    half = ROPE_DIM // 2
    if d < half:
        return dst + half
    return dst - half if d < ROPE_DIM else None


def _attn_tables():
    half = ROPE_DIM // 2
    pos = jnp.arange(SEQ, dtype=F32)
    inv_freq = ROPE_THETA ** (-jnp.arange(0, ROPE_DIM, 2, dtype=F32) / ROPE_DIM)
    phase = pos[:, None] * inv_freq[None, :]
    cos, sin = jnp.cos(phase), jnp.sin(phase)
    pad = ATTN_HEAD_DIM - ROPE_DIM
    c = jnp.concatenate([cos, cos, jnp.ones((SEQ, pad), F32)], axis=1)
    s = jnp.concatenate([-sin, sin, jnp.zeros((SEQ, pad), F32)], axis=1)
    return jnp.tile(c, (1, 2)), jnp.tile(s, (1, 2))


def _ret_tables():
    pos = jnp.arange(SEQ, dtype=F32)
    angle = 1.0 / (RET_THETA ** jnp.linspace(0.0, 1.0, RET_KEY_DIM // 2, dtype=F32))
    angle = jnp.repeat(angle, 2)
    phase = pos[:, None] * angle[None, :]
    cos, sin = jnp.cos(phase), jnp.sin(phase)
    even = (jnp.arange(RET_KEY_DIM) % 2 == 0)[None, :]
    return jnp.tile(cos, (1, 2)), jnp.tile(jnp.where(even, -sin, sin), (1, 2))


def _split_w_in(w):
    w_head = w[:, :, :W_HEAD].astype(BF16)
    w_tail = w[:, :, W_TAIL_START:W_TAIL_START + W_TAIL].astype(BF16)
    ab = w[:, :, W_HEAD:W_TAIL_START]
    w_ab = jnp.concatenate([ab, jnp.zeros((DEPTH, D_MODEL, LANES - 2 * DN_HEADS), F32)], axis=2).astype(BF16)
    return w_head, w_tail, w_ab, jnp.transpose(ab, (0, 2, 1)).astype(BF16)


def kernel(x, c, w_mod, b_mod, ffn1_norm, ffn1_w13, ffn1_w2, mix_norm, w_in, attn_q_norm, attn_k_norm,
           attn_sinks, dn_conv, dn_a_log, dn_dt_bias, dn_out_norm, w_branch, w_out, ffn2_norm, ffn2_w13,
           ffn2_w2):
    mod5 = _mod_call(c, w_mod, b_mod).reshape(DEPTH, BATCH, 3, 3, D_MODEL)
    attn_tabs = _attn_tables()
    ret_tabs = _ret_tables()
    bd = _block_diag_ones(MXU_TILE, ATTN_HEAD_DIM)
    rope_perm = _partner_matrix(_rope_partner)
    ret_perm = _partner_matrix(lambda dst: dst + 1 if dst % 2 == 0 else dst - 1)

    w13_1, w2_1 = ffn1_w13.astype(BF16), ffn1_w2.astype(BF16)
    w13_2, w2_2 = ffn2_w13.astype(BF16), ffn2_w2.astype(BF16)
    w_head, w_tail, w_ab, w_abt = _split_w_in(w_in)
    w_branch_b, w_out_b = w_branch.astype(BF16), w_out.astype(BF16)
    gain1, gain_mix, gain2 = (g.reshape(DEPTH, 1, D_MODEL) for g in (ffn1_norm, mix_norm, ffn2_norm))
    qk_gain = jnp.concatenate([jnp.tile(attn_q_norm, (1, ATTN_Q_HEADS)),
                               jnp.tile(attn_k_norm, (1, ATTN_KV_HEADS))], axis=1)[:, None, :]
    sinks = jnp.broadcast_to(attn_sinks[:, :, None], (DEPTH, ATTN_Q_HEADS, LANES))
    lane_pad = ((0, 0), (0, LANES - DN_HEADS))
    alog = jnp.pad(dn_a_log, lane_pad)[:, None, :]
    dtb = jnp.pad(dn_dt_bias, lane_pad)[:, None, :]
    row_pad = ((0, 0), (0, DN_HEADS))
    alog_t = jnp.broadcast_to(jnp.pad(dn_a_log, row_pad)[:, :, None], (DEPTH, 2 * DN_HEADS, SEQ_TILE))
    dtb_t = jnp.broadcast_to(jnp.pad(dn_dt_bias, row_pad)[:, :, None], (DEPTH, 2 * DN_HEADS, SEQ_TILE))
    onorm = jnp.tile(dn_out_norm, (1, DN_HEADS))[:, None, :]

    xf = x.reshape(N_TOK, D_MODEL)
    for l in range(DEPTH):
        xf = _ffn_call(xf, mod5, l, 0, gain1, w13_1, w2_1)
        proj, ab, abt = _inproj_call(xf, mod5, l, gain_mix, w_head, w_tail, w_ab, w_abt)
        proj3 = proj.reshape(BATCH, SEQ, N_PROJ)
        out_a = _attn_call(proj3, l, attn_tabs, qk_gain, bd, rope_perm, sinks)
        out_b = _dn_call(proj3, ab, abt, l, dn_conv, alog, dtb, alog_t, dtb_t, onorm, bd)
        out_c = _ret_call(proj3, ret_tabs, ret_perm)
        xf = _merge_ffn_call(xf, mod5, l, out_a.reshape(N_TOK, 512), out_b.reshape(N_TOK, 512),
                             out_c.reshape(N_TOK, 512), proj, w_branch_b, w_out_b, gain2, w13_2, w2_2)
    return xf.reshape(BATCH, SEQ, D_MODEL)
```

```python
import math

import numpy as np
import jax
import jax.numpy as jnp
from jax import lax
from jax.experimental import pallas as pl
from jax.experimental.pallas import tpu as pltpu

F32 = jnp.float32
BF16 = jnp.bfloat16
HIGHEST = lax.Precision.HIGHEST

D_MODEL = 1024
BATCH = 8
SEQ = 2048
DEPTH = 2
N_TOK = BATCH * SEQ
N_MOD = 9
D_FF = 2816
EPS = 1e-6
NEG_INF = -1e30

ATTN_Q_HEADS = 8
ATTN_KV_HEADS = 2
ATTN_HEAD_DIM = 64
ATTN_BLOCK = 128
ROPE_DIM = 16
ROPE_THETA = 500000.0
DN_HEADS = 8
DN_DIM = 64
DN_CONV = 4
RET_HEADS = 4
RET_KEY_DIM = 64
RET_THETA = 10000.0

LANES = 128
SUBLANES = 8
MXU_TILE = 256

P_GATES = 0
P_RET = 3072
P_DQKV = 4608
P_Z = 6144
P_ATTN = 6912
N_PROJ = 7680
ATTN_W = 768
ATTN_QK = 640
W_HEAD = 2304
W_TAIL_START = 2320
W_TAIL = 5120
PROJ_PIECES = (
    (True, 2048, 3072, P_GATES),
    (True, 512, 1536, P_RET),
    (False, 768, 1536, P_DQKV),
    (True, 0, 512, P_Z),
    (False, 0, 768, P_ATTN),
)

TM = 512
FFN_CHUNK = 256
N_FFN_CHUNK = D_FF // FFN_CHUNK
PROJ_CHUNK = 768
MOD_TN = 1152
SEQ_TILE = 128
RET_TILE = 256
ATTN_TILE = 1024
DN_CHUNK = 64
VMEM_LIMIT = 56 * 1024 * 1024


def _dot(a, b, precision=None):
    return jnp.dot(a, b, preferred_element_type=F32, precision=precision)


def _dot_nt(a, b):
    return lax.dot_general(a, b, (((1,), (1,)), ((), ())), preferred_element_type=F32)


def _dot_tn(a, b):
    return lax.dot_general(a, b, (((0,), (0,)), ((), ())), preferred_element_type=F32)


def _split_bf16(a):
    hi = a.astype(BF16)
    return hi, (a - hi.astype(F32)).astype(BF16)


def _split3_bf16(a):
    h1 = a.astype(BF16)
    r1 = a - h1.astype(F32)
    h2 = r1.astype(BF16)
    h3 = (r1 - h2.astype(F32)).astype(BF16)
    return h1, h2, h3


def _dot_exact_rhs(m01, x):
    m = m01.astype(BF16)
    h1, h2, h3 = _split3_bf16(x)
    return _dot(m, h1) + (_dot(m, h2) + _dot(m, h3))


def _dot_exact_lhs(x, m01):
    m = m01.astype(BF16)
    h1, h2, h3 = _split3_bf16(x)
    return _dot(h1, m) + (_dot(h2, m) + _dot(h3, m))


def _sigmoid(x):
    return 0.5 * (jnp.tanh(0.5 * x) + 1.0)


def _silu(x):
    return x * _sigmoid(x)


def _modulate(x, gain, shift, scale):
    ms = jnp.mean(x * x, axis=-1, keepdims=True)
    return x * lax.rsqrt(ms + EPS) * (gain * (1.0 + scale)) + shift


def _lane_block_matmul(pieces, m):
    blk = m.shape[0]
    width = pieces[0].shape[-1]
    outs = []
    for s in range(0, width, blk):
        n = min(blk, width - s)
        mm = m if n == blk else m[:n, :n]
        acc = _dot(pieces[0][:, s:s + n], mm)
        for piece in pieces[1:]:
            acc = acc + _dot(piece[:, s:s + n], mm)
        outs.append(acc)
    return outs[0] if len(outs) == 1 else jnp.concatenate(outs, axis=1)


def _seg_sum(x2, bd):
    return _lane_block_matmul([x2.astype(BF16)], bd)


def _tile_lanes(t, width):
    reps = width // t.shape[-1]
    return t if reps == 1 else jnp.concatenate([t] * reps, axis=1)


def _rotary(x, perm, cos, sin):
    w = x.shape[-1]
    return x * _tile_lanes(cos, w) + _lane_block_matmul(list(_split_bf16(x)), perm) * _tile_lanes(sin, w)


def _cparams(*sem):
    return pltpu.CompilerParams(dimension_semantics=sem, vmem_limit_bytes=VMEM_LIMIT)


def _resident(shape):
    zeros = (0,) * len(shape)
    return pl.BlockSpec(shape, lambda *_: zeros, pipeline_mode=pl.Buffered(1))


def _layer_resident(shape, layer):
    idx = (layer,) + (0,) * len(shape)
    return pl.BlockSpec((1,) + tuple(shape), lambda *_: idx, pipeline_mode=pl.Buffered(1))


def _mod_spec(layer, group, batch_of):
    return pl.BlockSpec((1, 1, 1, 3, D_MODEL), lambda *g: (layer, batch_of(*g), group, 0, 0))


def _mod_kernel(c_ref, w_ref, b_ref, o_ref):
    c = c_ref[...]
    o_ref[0] = _dot(_silu(c), w_ref[0], precision=HIGHEST) + b_ref[0]


def _mod_call(c, w_mod, b_mod):
    n = N_MOD * D_MODEL
    return pl.pallas_call(
        _mod_kernel,
        grid=(DEPTH, n // MOD_TN),
        in_specs=[
            pl.BlockSpec((BATCH, D_MODEL), lambda l, j: (0, 0)),
            pl.BlockSpec((1, D_MODEL, MOD_TN), lambda l, j: (l, 0, j)),
            pl.BlockSpec((1, 1, MOD_TN), lambda l, j: (l, 0, j)),
        ],
        out_specs=pl.BlockSpec((1, BATCH, MOD_TN), lambda l, j: (l, 0, j)),
        out_shape=jax.ShapeDtypeStruct((DEPTH, BATCH, n), F32),
        compiler_params=_cparams("parallel", "parallel"),
        name="adaln_mod",
    )(c, w_mod, b_mod.reshape(DEPTH, 1, n))


def _ffn_half_step(x, mod, gain, w13_ref, w2_ref, h_ref):
    ub = _modulate(x, gain, mod[0:1], mod[1:2]).astype(BF16)
    for j in range(N_FFN_CHUNK):
        lo = j * FFN_CHUNK
        g = _dot(ub, w13_ref[0, :, lo:lo + FFN_CHUNK])
        up = _dot(ub, w13_ref[0, :, D_FF + lo:D_FF + lo + FFN_CHUNK])
        h_ref[:, lo:lo + FFN_CHUNK] = (_silu(g) * up).astype(BF16)
    y = _dot(h_ref[...], w2_ref[0])
    return x + (0.5 * mod[2:3]) * y


def _ffn_kernel(x_ref, mod_ref, gain_ref, w13_ref, w2_ref, o_ref, h_ref):
    o_ref[...] = _ffn_half_step(x_ref[...], mod_ref[0, 0, 0], gain_ref[0], w13_ref, w2_ref, h_ref)


def _ffn_call(x2d, mod5, layer, group, gains, w13, w2):
    tiles_per_batch = SEQ // TM
    return pl.pallas_call(
        _ffn_kernel,
        grid=(N_TOK // TM,),
        in_specs=[
            pl.BlockSpec((TM, D_MODEL), lambda i: (i, 0)),
            _mod_spec(layer, group, lambda i: i // tiles_per_batch),
            _layer_resident((1, D_MODEL), layer),
            _layer_resident((D_MODEL, 2 * D_FF), layer),
            _layer_resident((D_FF, D_MODEL), layer),
        ],
        out_specs=pl.BlockSpec((TM, D_MODEL), lambda i: (i, 0)),
        out_shape=jax.ShapeDtypeStruct((N_TOK, D_MODEL), F32),
        scratch_shapes=[pltpu.VMEM((TM, D_FF), BF16)],
        compiler_params=_cparams("parallel"),
        name="swiglu_ffn",
    )(x2d, mod5, gains, w13, w2)


def _inproj_kernel(x_ref, mod_ref, gain_ref, wh_ref, wt_ref, wab_ref, wabt_ref, proj_ref, ab_ref, abt_ref):
    mod = mod_ref[0, 0, 0]
    ub = _modulate(x_ref[...], gain_ref[0], mod[0:1], mod[1:2]).astype(BF16)
    for from_tail, src, width, dst in PROJ_PIECES:
        w_ref = wt_ref if from_tail else wh_ref
        for off in range(0, width, PROJ_CHUNK):
            n = min(PROJ_CHUNK, width - off)
            proj_ref[:, dst + off:dst + off + n] = _dot(ub, w_ref[0, :, src + off:src + off + n]).astype(BF16)
    proj_ref[:, P_Z + 512:P_ATTN] = jnp.zeros((TM, P_ATTN - P_Z - 512), BF16)
    ab_ref[...] = _dot(ub, wab_ref[0])
    abt_ref[0] = _dot_nt(wabt_ref[0], ub)


def _inproj_call(x2d, mod5, layer, gains, w_head, w_tail, w_ab, w_abt):
    tiles_per_batch = SEQ // TM
    return pl.pallas_call(
        _inproj_kernel,
        grid=(N_TOK // TM,),
        in_specs=[
            pl.BlockSpec((TM, D_MODEL), lambda i: (i, 0)),
            _mod_spec(layer, 1, lambda i: i // tiles_per_batch),
            _layer_resident((1, D_MODEL), layer),
            _layer_resident((D_MODEL, W_HEAD), layer),
            _layer_resident((D_MODEL, W_TAIL), layer),
            _layer_resident((D_MODEL, LANES), layer),
            _layer_resident((2 * DN_HEADS, D_MODEL), layer),
        ],
        out_specs=[
            pl.BlockSpec((TM, N_PROJ), lambda i: (i, 0)),
            pl.BlockSpec((TM, LANES), lambda i: (i, 0)),
            pl.BlockSpec((1, 2 * DN_HEADS, TM), lambda i: (i // tiles_per_batch, 0, i % tiles_per_batch)),
        ],
        out_shape=[
            jax.ShapeDtypeStruct((N_TOK, N_PROJ), BF16),
            jax.ShapeDtypeStruct((N_TOK, LANES), F32),
            jax.ShapeDtypeStruct((BATCH, 2 * DN_HEADS, SEQ), F32),
        ],
        compiler_params=_cparams("parallel"),
        name="mixer_in_proj",
    )(x2d, mod5, gains, w_head, w_tail, w_ab, w_abt)


def _attn_kernel(cur_ref, prev_ref, tc_ref, ts_ref, pc_ref, ps_ref, gain_ref, bd_ref, perm_ref,
                 sink_ref, o_ref):
    t = pl.program_id(1)
    blk = ATTN_BLOCK
    nblk = ATTN_TILE // blk
    per_group = ATTN_Q_HEADS // ATTN_KV_HEADS
    bd = bd_ref[...]
    perm = perm_ref[...]
    gain = gain_ref[0]

    def norm_rope(x, g, cos, sin):
        x = x * lax.rsqrt(_seg_sum(x * x, bd) * (1.0 / ATTN_HEAD_DIM) + EPS) * g
        return _rotary(x, perm, cos, sin)

    cur = cur_ref[0]
    prv = prev_ref[0]
    qk = norm_rope(cur[:, :ATTN_QK].astype(F32), gain, tc_ref[...], ts_ref[...])
    kprev = norm_rope(prv[:, 512:ATTN_QK].astype(F32), gain[:, 512:], pc_ref[...], ps_ref[...])
    q = qk[:, :512] * (ATTN_HEAD_DIM ** -0.5)
    k_all = jnp.concatenate([kprev, qk[:, 512:]], axis=0)
    v_all = jnp.concatenate([prv[:, ATTN_QK:], cur[:, ATTN_QK:]], axis=0).astype(F32)

    low_half = lax.broadcasted_iota(jnp.int32, (1, LANES), 1) < ATTN_HEAD_DIM

    def group_on_both_halves(a, g):
        swapped = pltpu.roll(a, ATTN_HEAD_DIM, 1)
        return (jnp.where(low_half, a, swapped) if g == 0 else jnp.where(low_half, swapped, a)).astype(BF16)

    kd = [group_on_both_halves(k_all, g) for g in range(ATTN_KV_HEADS)]
    vd = [group_on_both_halves(v_all, g) for g in range(ATTN_KV_HEADS)]

    kj = lax.broadcasted_iota(jnp.int32, (2 * blk, blk), 0)
    qi = lax.broadcasted_iota(jnp.int32, (2 * blk, blk), 1)
    in_window = (kj > qi) & (kj <= qi + blk)
    bias = jnp.where(in_window, 0.0, NEG_INF)
    bias_no_prev = jnp.where(in_window & (kj >= blk), 0.0, NEG_INF)
    bias_rest = jnp.concatenate([bias] * per_group, axis=1)
    bias_first = jnp.where(t == 0, jnp.concatenate([bias_no_prev] * per_group, axis=1), bias_rest)
    sink_row = lax.broadcasted_iota(jnp.int32, (2 * blk, per_group * blk), 0) == 0
    sinks = sink_ref[0]
    ones_blk = jnp.ones((2 * blk, LANES), BF16)
    first_key = lax.broadcasted_iota(jnp.int32, (2 * blk, LANES), 0) == 0

    units = [(i, g) for i in range(nblk) for g in range(ATTN_KV_HEADS)]
    logits = []
    for i, g in units:
        rows = slice(i * blk, (i + 1) * blk)
        parts = []
        for p in range(g * per_group // 2, (g + 1) * per_group // 2):
            qp = q[rows, p * LANES:(p + 1) * LANES]
            parts += [jnp.where(low_half, qp, 0.0), jnp.where(low_half, 0.0, qp)]
        qs = jnp.concatenate(parts, axis=0).astype(BF16)
        band = kd[g][i * blk:(i + 2) * blk]
        sink = jnp.concatenate(
            [jnp.broadcast_to(sinks[g * per_group + r:g * per_group + r + 1, :], (2 * blk, LANES))
             for r in range(per_group)], axis=1)
        lg = _dot_nt(band, qs) + (bias_first if i == 0 else bias_rest)
        logits.append(jnp.where(sink_row, sink, lg))
    expd = []
    for lg in logits:
        m = jnp.max(lg, axis=0, keepdims=True)
        expd.append(jnp.exp(lg - m).astype(BF16))
    for (i, g), e in zip(units, expd):
        vals = jnp.where(first_key, 0.0, vd[g][i * blk:(i + 2) * blk].astype(F32)).astype(BF16)
        o = _dot_tn(e, jnp.concatenate([vals, ones_blk], axis=1))
        o = o[:, :LANES] * (1.0 / o[:, LANES:])
        for j in range(per_group // 2):
            p = g * per_group // 2 + j
            pair = jnp.where(low_half, o[2 * j * blk:(2 * j + 1) * blk], o[(2 * j + 1) * blk:(2 * j + 2) * blk])
            o_ref[0, i * blk:(i + 1) * blk, p * LANES:(p + 1) * LANES] = pair.astype(BF16)


def _attn_call(proj3, layer, tabs, gains, bd, perm, sinks):
    nblk = ATTN_TILE // ATTN_BLOCK
    col = P_ATTN // ATTN_W
    tab_cur = pl.BlockSpec((ATTN_TILE, LANES), lambda b, t: (t, 0))
    tab_prev = pl.BlockSpec((ATTN_BLOCK, LANES), lambda b, t: (jnp.maximum(t * nblk - 1, 0), 0))
    return pl.pallas_call(
        _attn_kernel,
        grid=(BATCH, SEQ // ATTN_TILE),
        in_specs=[
            pl.BlockSpec((1, ATTN_TILE, ATTN_W), lambda b, t: (b, t, col)),
            pl.BlockSpec((1, ATTN_BLOCK, ATTN_W), lambda b, t: (b, jnp.maximum(t * nblk - 1, 0), col)),
            tab_cur, tab_cur, tab_prev, tab_prev,
            _layer_resident((1, ATTN_QK), layer),
            _resident((MXU_TILE, MXU_TILE)),
            _resident((MXU_TILE, MXU_TILE)),
            _layer_resident((ATTN_Q_HEADS, LANES), layer),
        ],
        out_specs=pl.BlockSpec((1, ATTN_TILE, 512), lambda b, t: (b, t, 0)),
        out_shape=jax.ShapeDtypeStruct((BATCH, SEQ, 512), BF16),
        compiler_params=_cparams("parallel", "parallel"),
        name="swa_attention",
    )(proj3, proj3, tabs[0], tabs[1], tabs[0], tabs[1], gains, bd, perm, sinks)


def _softplus(x):
    return jnp.maximum(x, 0.0) + jnp.log(1.0 + jnp.exp(-jnp.abs(x)))


def _dn_kernel(cur_ref, prev_ref, z_ref, ab_ref, abt_ref, conv_ref, alog_ref, dtb_ref,
               alogt_ref, dtbt_ref, onorm_ref, bd_ref, o_ref, state_ref, y_ref, c_ref, r_ref):
    t = pl.program_id(1)
    T = SEQ_TILE
    C = DN_CHUNK
    nh = DN_HEADS
    width = nh * DN_DIM

    @pl.when(t == 0)
    def _():
        state_ref[...] = jnp.zeros_like(state_ref)

    x = cur_ref[0].astype(F32)
    tail = jnp.where(t > 0, prev_ref[0].astype(F32)[SUBLANES:], 0.0)
    xc = jnp.concatenate([tail, x], axis=0)
    w = conv_ref[0]
    y = xc[SUBLANES:SUBLANES + T] * w[3:4]
    for j in range(DN_CONV - 1):
        off = SUBLANES - (DN_CONV - 1) + j
        y = y + xc[off:off + T] * w[j:j + 1]
    y = _silu(y)
    bd = bd_ref[...]
    q = y[:, :width]
    k = y[:, width:2 * width]
    v = y[:, 2 * width:]
    q = q * lax.rsqrt(_seg_sum(q * q, bd) + EPS) * (DN_DIM ** -0.5)
    k = k * lax.rsqrt(_seg_sum(k * k, bd) + EPS)

    ab = ab_ref[...]
    la = -jnp.exp(alog_ref[0]) * _softplus(ab + dtb_ref[0])
    beta = _sigmoid(ab)
    lat = -jnp.exp(alogt_ref[0]) * _softplus(abt_ref[0] + dtbt_ref[0])
    ri = lax.broadcasted_iota(jnp.int32, (T, T), 0)
    ci = lax.broadcasted_iota(jnp.int32, (T, T), 1)
    same = (ri // C) == (ci // C)
    causal = same & (ci <= ri)
    strict = same & (ci < ri)
    eye = (ri == ci).astype(F32)
    g_nat = _dot_exact_rhs(causal, la)
    g_t = _dot_exact_lhs(lat, same & (ri <= ci))
    gend_nat = jnp.concatenate(
        [jnp.broadcast_to(g_nat[c * C + C - 1:c * C + C, :], (C, LANES)) for c in range(T // C)], axis=0)

    low_half = lax.broadcasted_iota(jnp.int32, (1, LANES), 1) < DN_DIM
    high_half = jnp.logical_not(low_half)
    bdr = lax.broadcasted_iota(jnp.int32, (LANES, LANES), 0) // DN_DIM
    bdc = lax.broadcasted_iota(jnp.int32, (LANES, LANES), 1) // DN_DIM
    blockdiag = bdr == bdc

    def per_head_lanes(src, first):
        return jnp.where(low_half, src[:, first:first + 1], src[:, first + 1:first + 2])

    npair = nh // 2
    halves = (low_half, high_half)
    qd, kt, dec_end, rhs, pw, xinv, intras = [], [], [], [], [], [], []
    for p in range(npair):
        sl = slice(p * LANES, (p + 1) * LANES)
        gx = per_head_lanes(g_nat, 2 * p)
        gendx = per_head_lanes(gend_nat, 2 * p)
        bx = per_head_lanes(beta, nh + 2 * p)
        eg = jnp.exp(gx)
        k_p = k[:, sl]
        q_p = q[:, sl]
        kb = k_p * bx
        vb = v[:, sl] * bx
        kbe = kb * eg
        qd.append((q_p * eg).astype(BF16))
        kt.append((k_p * jnp.exp(gendx - gx)).astype(BF16))
        dec_end.append(jnp.exp(gendx))
        k_pb = k_p.astype(BF16)
        for half in range(2):
            h = 2 * p + half
            mh = halves[half]
            diff = g_nat[:, h:h + 1] - g_t[h:h + 1, :]
            decay = jnp.exp(jnp.where(causal, diff, NEG_INF))
            kk = _dot_nt(jnp.where(mh, kb, 0.0).astype(BF16), k_pb)
            qkm = _dot_nt(jnp.where(mh, q_p, 0.0).astype(BF16), k_pb)
            intras.append((qkm * decay).astype(BF16))
            n1 = jnp.where(strict, -(kk * decay), 0.0).astype(BF16)
            pw.append(n1)
            xinv.append(eye + n1.astype(F32))
            rhs.append(jnp.concatenate([jnp.where(mh, vb, 0.0), jnp.where(mh, kbe, 0.0)], axis=1))
    n1s = list(pw)
    for _ in range(5):
        for h in range(nh):
            pw[h] = _dot(pw[h], pw[h]).astype(BF16)
            xinv[h] = xinv[h] + _dot(xinv[h].astype(BF16), pw[h])
    xb = [xinv[h].astype(BF16) for h in range(nh)]
    one = bd[0:1, 0:1].astype(F32)
    for h in range(nh):
        y_ref[h] = _dot(xb[h], rhs[h].astype(BF16))
    for h in range(nh):
        yh, yl = _split_bf16(y_ref[h])
        c_ref[h] = _dot(n1s[h], yh) + _dot(n1s[h], yl)
    for h in range(nh):
        r_ref[h] = ((rhs[h] - y_ref[h]) + c_ref[h]).astype(BF16)
    for h in range(nh):
        c_ref[h] = _dot(xb[h], r_ref[h])
    for h in range(nh):
        y_ref[h] = y_ref[h] + one * c_ref[h]
    u, w_ = [], []
    for p in range(npair):
        u.append(y_ref[2 * p, :, :LANES] + y_ref[2 * p + 1, :, :LANES])
        w_.append((y_ref[2 * p, :, LANES:] + y_ref[2 * p + 1, :, LANES:]).astype(BF16))
    state = [state_ref[p] for p in range(npair)]
    vn = [[] for _ in range(npair)]
    oi = [[] for _ in range(npair)]
    for c in range(T // C):
        rows = slice(c * C, (c + 1) * C)
        for p in range(npair):
            sb = state[p].astype(BF16)
            vn_c = u[p][rows] - _dot(w_[p][rows], sb)
            oi[p].append(_dot(qd[p][rows], sb))
            upd = _dot_tn(kt[p][rows], vn_c.astype(BF16))
            state[p] = state[p] * dec_end[p][c * C:c * C + 1] + jnp.where(blockdiag, upd, 0.0)
            vn[p].append(vn_c)
    outs = []
    for p in range(npair):
        state_ref[p] = state[p]
        vn_all = jnp.concatenate(vn[p], axis=0)
        o_p = jnp.concatenate(oi[p], axis=0)
        for half in range(2):
            o_p = o_p + _dot(intras[2 * p + half], jnp.where(halves[half], vn_all, 0.0).astype(BF16))
        outs.append(o_p)
    o = jnp.concatenate(outs, axis=1)
    ms = _seg_sum(o * o, bd) * (1.0 / DN_DIM)
    zg = _silu(z_ref[0].astype(F32))
    o_ref[0] = (o * lax.rsqrt(ms + EPS) * onorm_ref[0] * zg).astype(BF16)


def _dn_call(proj3, ab, abt, layer, conv_w, alog, dtb, alog_t, dtb_t, onorm, bd):
    T = SEQ_TILE
    tail_rows = 2 * SUBLANES
    per_tile = T // tail_rows
    return pl.pallas_call(
        _dn_kernel,
        grid=(BATCH, SEQ // T),
        in_specs=[
            pl.BlockSpec((1, T, 1536), lambda b, t: (b, t, P_DQKV // 1536)),
            pl.BlockSpec((1, tail_rows, 1536), lambda b, t: (b, jnp.maximum(t * per_tile - 1, 0), P_DQKV // 1536)),
            pl.BlockSpec((1, T, 512), lambda b, t: (b, t, P_Z // 512)),
            pl.BlockSpec((T, LANES), lambda b, t: (b * (SEQ // T) + t, 0)),
            pl.BlockSpec((1, 2 * DN_HEADS, T), lambda b, t: (b, 0, t)),
            _layer_resident((DN_CONV, 1536), layer),
            _layer_resident((1, LANES), layer),
            _layer_resident((1, LANES), layer),
            _layer_resident((2 * DN_HEADS, T), layer),
            _layer_resident((2 * DN_HEADS, T), layer),
            _layer_resident((1, 512), layer),
            _resident((MXU_TILE, MXU_TILE)),
        ],
        out_specs=pl.BlockSpec((1, T, 512), lambda b, t: (b, t, 0)),
        out_shape=jax.ShapeDtypeStruct((BATCH, SEQ, 512), BF16),
        scratch_shapes=[pltpu.VMEM((DN_HEADS // 2, LANES, LANES), F32),
                        pltpu.VMEM((DN_HEADS, T, 2 * LANES), F32),
                        pltpu.VMEM((DN_HEADS, T, 2 * LANES), F32),
                        pltpu.VMEM((DN_HEADS, T, 2 * LANES), BF16)],
        compiler_params=_cparams("parallel", "arbitrary"),
        name="gated_deltanet",
    )(proj3, proj3, proj3, ab, abt, conv_w, alog, dtb, alog_t, dtb_t, onorm, bd)


def _ret_kernel(x_ref, tc_ref, ts_ref, perm_ref, o_ref, state_ref):
    t = pl.program_id(1)
    T = RET_TILE

    @pl.when(t == 0)
    def _():
        state_ref[...] = jnp.zeros_like(state_ref)

    x = x_ref[0]
    qk = _rotary(x[:, :512].astype(F32), perm_ref[...], tc_ref[...], ts_ref[...])
    rq = qk[:, :256]
    rk = qk[:, 256:] * (RET_KEY_DIM ** -0.5)
    ri = lax.broadcasted_iota(jnp.int32, (T, T), 0)
    ci = lax.broadcasted_iota(jnp.int32, (T, T), 1)
    causal = ci <= ri
    dist = (ri - ci).astype(F32)
    row = lax.broadcasted_iota(jnp.int32, (T, LANES), 0).astype(F32)
    low_half = lax.broadcasted_iota(jnp.int32, (1, LANES), 1) < RET_KEY_DIM

    log_gamma = [math.log1p(-(2.0 ** (-5.0 - h))) for h in range(RET_HEADS)]
    v = [x[:, 512 + h * LANES:512 + (h + 1) * LANES] for h in range(RET_HEADS)]
    scores, q_dec, k_end = [], [], []
    for h in range(RET_HEADS):
        p = h // 2
        mh = low_half if h % 2 == 0 else jnp.logical_not(low_half)
        q_p = rq[:, p * LANES:(p + 1) * LANES]
        k_p = rk[:, p * LANES:(p + 1) * LANES]
        decay = jnp.exp(jnp.where(causal, dist * log_gamma[h], NEG_INF))
        scores.append((_dot_nt(jnp.where(mh, q_p, 0.0).astype(BF16), k_p.astype(BF16)) * decay).astype(BF16))
        q_dec.append(jnp.where(mh, q_p * jnp.exp(log_gamma[h] * (row + 1.0)), 0.0).astype(BF16))
        k_end.append(jnp.where(mh, k_p * jnp.exp(log_gamma[h] * (T - 1.0 - row)), 0.0).astype(BF16))
    outs = []
    for h in range(RET_HEADS):
        state = state_ref[h]
        outs.append(_dot(scores[h], v[h]) + _dot(q_dec[h], state.astype(BF16)))
        state_ref[h] = state * math.exp(log_gamma[h] * T) + _dot_tn(k_end[h], v[h])
    for h in range(RET_HEADS):
        out = outs[h]
        ms = jnp.mean(out * out, axis=-1, keepdims=True)
        g = _silu(x[:, 1024 + h * LANES:1024 + (h + 1) * LANES].astype(F32))
        o_ref[0, :, h * LANES:(h + 1) * LANES] = (out * lax.rsqrt(ms + EPS) * g).astype(BF16)


def _ret_call(proj3, tabs, perm):
    T = RET_TILE
    tab = pl.BlockSpec((T, LANES), lambda b, t: (t, 0))
    return pl.pallas_call(
        _ret_kernel,
        grid=(BATCH, SEQ // T),
        in_specs=[pl.BlockSpec((1, T, 1536), lambda b, t: (b, t, P_RET // 1536)), tab, tab,
                  _resident((MXU_TILE, MXU_TILE))],
        out_specs=pl.BlockSpec((1, T, 512), lambda b, t: (b, t, 0)),
        out_shape=jax.ShapeDtypeStruct((BATCH, SEQ, 512), BF16),
        scratch_shapes=[pltpu.VMEM((RET_HEADS, LANES, LANES), F32)],
        compiler_params=_cparams("parallel", "arbitrary"),
        name="retention",
    )(proj3, tabs[0], tabs[1], perm)


def _merge_ffn_kernel(x_ref, mod_mix_ref, mod_ffn_ref, a_ref, b_ref, c_ref, g_ref, wb_ref, wo_ref,
                      gain_ref, w13_ref, w2_ref, o_ref, h_ref):
    merged = jnp.zeros((TM, D_MODEL), F32)
    for i, br in enumerate((a_ref, b_ref, c_ref)):
        gate = _sigmoid(g_ref[:, i * D_MODEL:(i + 1) * D_MODEL].astype(F32))
        merged = merged + gate * _dot(br[...], wb_ref[0, i])
    y = _dot(merged.astype(BF16), wo_ref[0])
    x = x_ref[...] + mod_mix_ref[0, 0, 0][2:3] * y
    o_ref[...] = _ffn_half_step(x, mod_ffn_ref[0, 0, 0], gain_ref[0], w13_ref, w2_ref, h_ref)


def _merge_ffn_call(x2d, mod5, layer, out_a, out_b, out_c, proj2, w_branch, w_out, gains, w13, w2):
    tiles_per_batch = SEQ // TM
    batch_of = lambda i: i // tiles_per_batch
    branch = pl.BlockSpec((TM, 512), lambda i: (i, 0))
    return pl.pallas_call(
        _merge_ffn_kernel,
        grid=(N_TOK // TM,),
        in_specs=[
            pl.BlockSpec((TM, D_MODEL), lambda i: (i, 0)),
            _mod_spec(layer, 1, batch_of),
            _mod_spec(layer, 2, batch_of),
            branch, branch, branch,
            pl.BlockSpec((TM, 3 * D_MODEL), lambda i: (i, P_GATES // (3 * D_MODEL))),
            _layer_resident((3, 512, D_MODEL), layer),
            _layer_resident((D_MODEL, D_MODEL), layer),
            _layer_resident((1, D_MODEL), layer),
            _layer_resident((D_MODEL, 2 * D_FF), layer),
            _layer_resident((D_FF, D_MODEL), layer),
        ],
        out_specs=pl.BlockSpec((TM, D_MODEL), lambda i: (i, 0)),
        out_shape=jax.ShapeDtypeStruct((N_TOK, D_MODEL), F32),
        scratch_shapes=[pltpu.VMEM((TM, D_FF), BF16)],
        compiler_params=_cparams("parallel"),
        name="merge_ffn",
    )(x2d, mod5, mod5, out_a, out_b, out_c, proj2, w_branch, w_out, gains, w13, w2)


def _block_diag_ones(n, seg):
    idx = np.arange(n) // seg
    return jnp.asarray((idx[:, None] == idx[None, :]).astype(np.float32), dtype=BF16)


def _partner_matrix(partner_of):
    m = np.zeros((MXU_TILE, MXU_TILE), np.float32)
    for dst in range(MXU_TILE):
        src = partner_of(dst)
        if src is not None:
            m[src, dst] = 1.0
    return jnp.asarray(m, dtype=BF16)


def _rope_partner(dst):
    d = dst % ATTN_HEAD_DIM
    half = ROPE_DIM // 2
    if d < half:
        return dst + half
    return dst - half if d < ROPE_DIM else None


def _attn_tables():
    half = ROPE_DIM // 2
    pos = jnp.arange(SEQ, dtype=F32)
    inv_freq = ROPE_THETA ** (-jnp.arange(0, ROPE_DIM, 2, dtype=F32) / ROPE_DIM)
    phase = pos[:, None] * inv_freq[None, :]
    cos, sin = jnp.cos(phase), jnp.sin(phase)
    pad = ATTN_HEAD_DIM - ROPE_DIM
    c = jnp.concatenate([cos, cos, jnp.ones((SEQ, pad), F32)], axis=1)
    s = jnp.concatenate([-sin, sin, jnp.zeros((SEQ, pad), F32)], axis=1)
    return jnp.tile(c, (1, 2)), jnp.tile(s, (1, 2))


def _ret_tables():
    pos = jnp.arange(SEQ, dtype=F32)
    angle = 1.0 / (RET_THETA ** jnp.linspace(0.0, 1.0, RET_KEY_DIM // 2, dtype=F32))
    angle = jnp.repeat(angle, 2)
    phase = pos[:, None] * angle[None, :]
    cos, sin = jnp.cos(phase), jnp.sin(phase)
    even = (jnp.arange(RET_KEY_DIM) % 2 == 0)[None, :]
    return jnp.tile(cos, (1, 2)), jnp.tile(jnp.where(even, -sin, sin), (1, 2))


def _split_w_in(w):
    w_head = w[:, :, :W_HEAD].astype(BF16)
    w_tail = w[:, :, W_TAIL_START:W_TAIL_START + W_TAIL].astype(BF16)
    ab = w[:, :, W_HEAD:W_TAIL_START]
    w_ab = jnp.concatenate([ab, jnp.zeros((DEPTH, D_MODEL, LANES - 2 * DN_HEADS), F32)], axis=2).astype(BF16)
    return w_head, w_tail, w_ab, jnp.transpose(ab, (0, 2, 1)).astype(BF16)


def kernel(x, c, w_mod, b_mod, ffn1_norm, ffn1_w13, ffn1_w2, mix_norm, w_in, attn_q_norm, attn_k_norm,
           attn_sinks, dn_conv, dn_a_log, dn_dt_bias, dn_out_norm, w_branch, w_out, ffn2_norm, ffn2_w13,
           ffn2_w2):
    mod5 = _mod_call(c, w_mod, b_mod).reshape(DEPTH, BATCH, 3, 3, D_MODEL)
    attn_tabs = _attn_tables()
    ret_tabs = _ret_tables()
    bd = _block_diag_ones(MXU_TILE, ATTN_HEAD_DIM)
    rope_perm = _partner_matrix(_rope_partner)
    ret_perm = _partner_matrix(lambda dst: dst + 1 if dst % 2 == 0 else dst - 1)

    w13_1, w2_1 = ffn1_w13.astype(BF16), ffn1_w2.astype(BF16)
    w13_2, w2_2 = ffn2_w13.astype(BF16), ffn2_w2.astype(BF16)
    w_head, w_tail, w_ab, w_abt = _split_w_in(w_in)
    w_branch_b, w_out_b = w_branch.astype(BF16), w_out.astype(BF16)
    gain1, gain_mix, gain2 = (g.reshape(DEPTH, 1, D_MODEL) for g in (ffn1_norm, mix_norm, ffn2_norm))
    qk_gain = jnp.concatenate([jnp.tile(attn_q_norm, (1, ATTN_Q_HEADS)),
                               jnp.tile(attn_k_norm, (1, ATTN_KV_HEADS))], axis=1)[:, None, :]
    sinks = jnp.broadcast_to(attn_sinks[:, :, None], (DEPTH, ATTN_Q_HEADS, LANES))
    lane_pad = ((0, 0), (0, LANES - DN_HEADS))
    alog = jnp.pad(dn_a_log, lane_pad)[:, None, :]
    dtb = jnp.pad(dn_dt_bias, lane_pad)[:, None, :]
    row_pad = ((0, 0), (0, DN_HEADS))
    alog_t = jnp.broadcast_to(jnp.pad(dn_a_log, row_pad)[:, :, None], (DEPTH, 2 * DN_HEADS, SEQ_TILE))
    dtb_t = jnp.broadcast_to(jnp.pad(dn_dt_bias, row_pad)[:, :, None], (DEPTH, 2 * DN_HEADS, SEQ_TILE))
    onorm = jnp.tile(dn_out_norm, (1, DN_HEADS))[:, None, :]

    xf = x.reshape(N_TOK, D_MODEL)
    for l in range(DEPTH):
        xf = _ffn_call(xf, mod5, l, 0, gain1, w13_1, w2_1)
        proj, ab, abt = _inproj_call(xf, mod5, l, gain_mix, w_head, w_tail, w_ab, w_abt)
        proj3 = proj.reshape(BATCH, SEQ, N_PROJ)
        out_a = _attn_call(proj3, l, attn_tabs, qk_gain, bd, rope_perm, sinks)
        out_b = _dn_call(proj3, ab, abt, l, dn_conv, alog, dtb, alog_t, dtb_t, onorm, bd)
        out_c = _ret_call(proj3, ret_tabs, ret_perm)
        xf = _merge_ffn_call(xf, mod5, l, out_a.reshape(N_TOK, 512), out_b.reshape(N_TOK, 512),
                             out_c.reshape(N_TOK, 512), proj, w_branch_b, w_out_b, gain2, w13_2, w2_2)
    return xf.reshape(BATCH, SEQ, D_MODEL)
```
